```python
import jax, jax.numpy as jnp
from jax import lax
import numpy as np

D_MODEL = 1024
BATCH = 32
SEQ = 2048
DEPTH = 2

GRID_W = 64
CTX_LEN = 256
HEAD_DIM = 64
AXIS_DIM = HEAD_DIM // 2
ROPE_THETA = 10000.0
A_HEADS = 6
A_KV = 2
A_GROUP = A_HEADS // A_KV
C_HEADS = 6
C_KV = 2
C_GROUP = C_HEADS // C_KV
POOL_GROUPS = 4
POOL_CH = 64
POOL_WIDTH = POOL_GROUPS * POOL_CH
POOL_WINDOWS = (2, 4, 8, 16)
WINDOW = 128
Q_BLOCK = 128
BAND = Q_BLOCK + 2 * WINDOW
D_FF = 4 * D_MODEL
N_BRANCH = 3
A_QW = A_HEADS * HEAD_DIM
A_KVW = A_KV * HEAD_DIM
C_QW = C_HEADS * HEAD_DIM
C_KVW = C_KV * HEAD_DIM
IN_SPLITS = (A_QW, A_KVW, A_KVW, C_QW, C_KVW, C_KVW, POOL_WIDTH, D_MODEL, D_MODEL, D_MODEL)
IN_WIDTH = A_QW + 2 * A_KVW + C_QW + 2 * C_KVW + POOL_WIDTH + N_BRANCH * D_MODEL
EPS = 1e-6
NEG = -1e30

kernel_name = "hybrid_prefix_gqa_pool_window_block"


def rmsnorm(x, g):
    xf = x.astype(jnp.float32)
    y = xf * lax.rsqrt(jnp.mean(xf * xf, axis=-1, keepdims=True) + EPS)
    return (y * g.astype(jnp.float32)).astype(x.dtype)


def modulate(x, g, shift, scale):
    return rmsnorm(x, g) * (1 + scale) + shift


def adaln(v, w, b):
    m = jax.nn.silu(v) @ w + b
    return jnp.split(m, 6, axis=-1)


def split_in(z):
    idx = []
    acc = 0
    for s in IN_SPLITS[:-1]:
        acc += s
        idx.append(acc)
    return jnp.split(z, idx, axis=-1)


def heads_q(t, n_kv, n_group):
    b, l, _ = t.shape
    return t.reshape(b, l, n_kv, n_group, HEAD_DIM)


def heads_kv(t, n_kv):
    b, l, _ = t.shape
    return t.reshape(b, l, n_kv, HEAD_DIM)


def rope_tables(n_tok):
    rows = n_tok // GRID_W
    r = jnp.repeat(jnp.arange(rows, dtype=jnp.float32), GRID_W)
    col = jnp.tile(jnp.arange(GRID_W, dtype=jnp.float32), rows)
    inv = 1.0 / (ROPE_THETA ** (jnp.arange(0, AXIS_DIM, 2, dtype=jnp.float32) / AXIS_DIM))
    ang = jnp.concatenate([r[:, None] * inv, col[:, None] * inv], axis=-1)
    return jnp.cos(ang), jnp.sin(ang)


def apply_rope(x, cos, sin):
    shp = x.shape
    xr = x.reshape(shp[:-1] + (shp[-1] // 2, 2))
    x0, x1 = xr[..., 0], xr[..., 1]
    bshape = (shp[1],) + (1,) * (x.ndim - 3) + (shp[-1] // 2,)
    cs = cos.reshape(bshape).astype(x.dtype)
    sn = sin.reshape(bshape).astype(x.dtype)
    return jnp.stack([x0 * cs - x1 * sn, x0 * sn + x1 * cs], axis=-1).reshape(shp)


def global_attn(q, k, v):
    b, s, hk, g, dh = q.shape
    nblk = s // Q_BLOCK
    qb = q.reshape(b, nblk, Q_BLOCK, hk, g, dh).transpose(1, 0, 2, 3, 4, 5)
    scale = dh ** -0.5

    def one_block(qi):
        sc = jnp.einsum('bqhgd,bkhd->bhgqk', qi, k).astype(jnp.float32) * scale
        p = jax.nn.softmax(sc, axis=-1).astype(v.dtype)
        return jnp.einsum('bhgqk,bkhd->bqhgd', p, v)

    o = lax.map(one_block, qb)
    return o.transpose(1, 0, 2, 3, 4, 5).reshape(b, s, hk * g * dh)


def window_attn(q, k, v, kc, vc, sink):
    b, s, hk, g, dh = q.shape
    nctx = kc.shape[1]
    nblk = s // Q_BLOCK
    scale = dh ** -0.5
    k_pad = jnp.pad(k, ((0, 0), (WINDOW, WINDOW), (0, 0), (0, 0)))
    v_pad = jnp.pad(v, ((0, 0), (WINDOW, WINDOW), (0, 0), (0, 0)))
    qb = q.reshape(b, nblk, Q_BLOCK, hk, g, dh).transpose(1, 0, 2, 3, 4, 5)
    sink_f = sink.astype(jnp.float32).reshape(1, hk, g, 1, 1)

    def one_block(args):
        qi, bi = args
        start = bi * Q_BLOCK
        kb = lax.dynamic_slice_in_dim(k_pad, start, BAND, axis=1)
        vb = lax.dynamic_slice_in_dim(v_pad, start, BAND, axis=1)
        qpos = start + jnp.arange(Q_BLOCK)
        kpos = start - WINDOW + jnp.arange(BAND)
        valid = (jnp.abs(qpos[:, None] - kpos[None, :]) <= WINDOW) & (kpos[None, :] >= 0) & (kpos[None, :] < s)
        s_loc = jnp.einsum('bqhgd,bkhd->bhgqk', qi, kb).astype(jnp.float32) * scale
        s_loc = jnp.where(valid, s_loc, NEG)
        s_ctx = jnp.einsum('bqhgd,bkhd->bhgqk', qi, kc).astype(jnp.float32) * scale
        s_sink = jnp.broadcast_to(sink_f, s_ctx.shape[:-1] + (1,))
        p = jax.nn.softmax(jnp.concatenate([s_ctx, s_loc, s_sink], axis=-1), axis=-1).astype(v.dtype)
        o = jnp.einsum('bhgqk,bkhd->bqhgd', p[..., :nctx], vc)
        return o + jnp.einsum('bhgqk,bkhd->bqhgd', p[..., nctx:nctx + BAND], vb)

    o = lax.map(one_block, (qb, jnp.arange(nblk)))
    return o.transpose(1, 0, 2, 3, 4, 5).reshape(b, s, hk * g * dh)


def ctx_attn(q, k, v, sink):
    b, n, hk, g, dh = q.shape
    sc = jnp.einsum('bqhgd,bkhd->bhgqk', q, k).astype(jnp.float32) * (dh ** -0.5)
    if sink is not None:
        sk = jnp.broadcast_to(sink.astype(jnp.float32).reshape(1, hk, g, 1, 1), sc.shape[:-1] + (1,))
        p = jax.nn.softmax(jnp.concatenate([sc, sk], axis=-1), axis=-1)[..., :-1]
    else:
        p = jax.nn.softmax(sc, axis=-1)
    o = jnp.einsum('bhgqk,bkhd->bqhgd', p.astype(v.dtype), v)
    return o.reshape(b, n, hk * g * dh)


def pool_mix(u, w_pool, pool_scale):
    b, n, _ = u.shape
    uf = u.astype(jnp.float32).reshape(b, n, POOL_GROUPS, POOL_CH)
    cs = jnp.concatenate([jnp.zeros((b, 1, POOL_GROUPS, POOL_CH), jnp.float32), jnp.cumsum(uf, axis=1)], axis=1)
    t = jnp.arange(n)
    outs = []
    for gi, w in enumerate(POOL_WINDOWS):
        lo = jnp.clip(t - w // 2, 0, n)
        hi = jnp.clip(t + w - w // 2, 0, n)
        tot = cs[:, hi, gi] - cs[:, lo, gi]
        cnt = (hi - lo).astype(jnp.float32)
        outs.append(tot / cnt[None, :, None] - uf[:, :, gi])
    pooled = jnp.stack(outs, axis=2).astype(u.dtype)
    mixed = jnp.einsum('blgc,gcd->blgd', pooled, w_pool).reshape(b, n, POOL_WIDTH)
    return mixed * pool_scale


def project(h, w_in, qn_a, kn_a, qn_c, kn_c):
    qa, ka, va, qc, kc, vc, u, ga, gb, gc = split_in(h @ w_in)
    qa = rmsnorm(heads_q(qa, A_KV, A_GROUP), qn_a)
    ka = rmsnorm(heads_kv(ka, A_KV), kn_a)
    va = heads_kv(va, A_KV)
    qc = rmsnorm(heads_q(qc, C_KV, C_GROUP), qn_c)
    kc = rmsnorm(heads_kv(kc, C_KV), kn_c)
    vc = heads_kv(vc, C_KV)
    return qa, ka, va, qc, kc, vc, u, ga, gb, gc


def merge(oa, ob, oc, ga, gb, gc, w_br_a, w_br_b, w_br_c, w_out):
    y = (jax.nn.sigmoid(ga) * (oa @ w_br_a)
         + jax.nn.sigmoid(gb) * (ob @ w_br_b)
         + jax.nn.sigmoid(gc) * (oc @ w_br_c))
    return y @ w_out


def mlp(h, w1, w2):
    return jnp.square(jax.nn.relu(h @ w1)) @ w2


def setup_inputs(seed: int = 0) -> dict:
    key = jax.random.key(seed)
    ks = jax.random.split(key, 24)
    f32 = jnp.float32
    nrm = lambda k, shp, s: jax.random.normal(k, shp, f32) * s
    return {
        "x": nrm(ks[0], (BATCH, SEQ, D_MODEL), 1.0),
        "c": nrm(ks[1], (BATCH, D_MODEL), 1.0),
        "ctx": nrm(ks[2], (BATCH, CTX_LEN, D_MODEL), 1.0),
        "c_ctx": nrm(ks[3], (D_MODEL,), 1.0),
        "w_ada": nrm(ks[4], (DEPTH, D_MODEL, 6 * D_MODEL), 0.5 * D_MODEL ** -0.5),
        "b_ada": nrm(ks[5], (DEPTH, 6 * D_MODEL), 0.02),
        "norm1": 1.0 + nrm(ks[6], (DEPTH, D_MODEL), 0.1),
        "norm2": 1.0 + nrm(ks[7], (DEPTH, D_MODEL), 0.1),
        "w_in": nrm(ks[8], (DEPTH, D_MODEL, IN_WIDTH), D_MODEL ** -0.5),
        "q_norm_a": 1.0 + nrm(ks[9], (DEPTH, HEAD_DIM), 0.1),
        "k_norm_a": 1.0 + nrm(ks[10], (DEPTH, HEAD_DIM), 0.1),
        "q_norm_c": 1.0 + nrm(ks[11], (DEPTH, HEAD_DIM), 0.1),
        "k_norm_c": 1.0 + nrm(ks[12], (DEPTH, HEAD_DIM), 0.1),
        "sink_c": nrm(ks[13], (DEPTH, C_HEADS), 0.5),
        "w_pool": nrm(ks[14], (DEPTH, POOL_GROUPS, POOL_CH, POOL_CH), POOL_CH ** -0.5),
        "pool_scale": 1.0 + nrm(ks[15], (DEPTH, POOL_WIDTH), 0.1),
        "w_br_a": nrm(ks[16], (DEPTH, A_QW, D_MODEL), A_QW ** -0.5),
        "w_br_b": nrm(ks[17], (DEPTH, POOL_WIDTH, D_MODEL), POOL_WIDTH ** -0.5),
        "w_br_c": nrm(ks[18], (DEPTH, C_QW, D_MODEL), C_QW ** -0.5),
        "w_out": nrm(ks[19], (DEPTH, D_MODEL, D_MODEL), D_MODEL ** -0.5),
        "w_mlp1": nrm(ks[20], (DEPTH, D_MODEL, D_FF), D_MODEL ** -0.5),
        "w_mlp2": nrm(ks[21], (DEPTH, D_FF, D_MODEL), D_FF ** -0.5),
    }


def reference(x, c, ctx, c_ctx, w_ada, b_ada, norm1, norm2, w_in, q_norm_a, k_norm_a, q_norm_c, k_norm_c,
              sink_c, w_pool, pool_scale, w_br_a, w_br_b, w_br_c, w_out, w_mlp1, w_mlp2):
    n_tok = x.shape[1]
    cos, sin = rope_tables(n_tok)
    xc = ctx
    for l in range(DEPTH):
        last = l == DEPTH - 1
        sh1, sc1, g1, sh2, sc2, g2 = [m[:, None, :] for m in adaln(c, w_ada[l], b_ada[l])]
        csh1, csc1, cg1, csh2, csc2, cg2 = adaln(c_ctx, w_ada[l], b_ada[l])

        hc = modulate(xc, norm1[l], csh1, csc1)
        cqa, cka, cva, cqc, ckc, cvc, cu, cga, cgb, cgc = project(
            hc, w_in[l], q_norm_a[l], k_norm_a[l], q_norm_c[l], k_norm_c[l])

        h = modulate(x, norm1[l], sh1, sc1)
        qa, ka, va, qc, kc, vc, u, ga, gb, gc = project(
            h, w_in[l], q_norm_a[l], k_norm_a[l], q_norm_c[l], k_norm_c[l])
        qa, ka = apply_rope(qa, cos, sin), apply_rope(ka, cos, sin)
        qc, kc = apply_rope(qc, cos, sin), apply_rope(kc, cos, sin)

        oa = global_attn(qa, jnp.concatenate([cka, ka], axis=1), jnp.concatenate([cva, va], axis=1))
        ob = pool_mix(u, w_pool[l], pool_scale[l])
        oc = window_attn(qc, kc, vc, ckc, cvc, sink_c[l])
        x = x + g1 * merge(oa, ob, oc, ga, gb, gc, w_br_a[l], w_br_b[l], w_br_c[l], w_out[l])
        x = x + g2 * mlp(modulate(x, norm2[l], sh2, sc2), w_mlp1[l], w_mlp2[l])

        if not last:
            coa = ctx_attn(cqa, cka, cva, None)
            cob = pool_mix(cu, w_pool[l], pool_scale[l])
            coc = ctx_attn(cqc, ckc, cvc, sink_c[l])
            xc = xc + cg1 * merge(coa, cob, coc, cga, cgb, cgc, w_br_a[l], w_br_b[l], w_br_c[l], w_out[l])
            xc = xc + cg2 * mlp(modulate(xc, norm2[l], csh2, csc2), w_mlp1[l], w_mlp2[l])
    return x
```

```python
import functools

import numpy as np
import jax
import jax.numpy as jnp
from jax import lax
from jax.experimental import pallas as pl
from jax.experimental.pallas import tpu as pltpu

D_MODEL = 1024
DEPTH = 2
GRID_W = 64
HEAD_DIM = 64
AXIS_DIM = HEAD_DIM // 2
ROPE_THETA = 10000.0
N_HEADS = 6
N_KV = 2
N_GROUP = N_HEADS // N_KV
Q_WIDTH = N_HEADS * HEAD_DIM
KV_WIDTH = N_KV * HEAD_DIM
POOL_WIDTH = 256
POOL_CH = 64
POOL_HALO = 8
WINDOW = 128
D_FF = 4 * D_MODEL
EPS = 1e-6
NEG = -1e30

LANES = 128
HALO_ROWS = 16
VMEM_LIMIT_BYTES = 56 * 1024 * 1024

F32 = jnp.float32
BF16 = jnp.bfloat16

_QA, _KA, _VA, _QC, _KC, _VC, _U, _GATES = 0, 384, 512, 640, 1024, 1152, 1280, 1536
PROJ_WIDTH = 2 * Q_WIDTH + 4 * KV_WIDTH + POOL_WIDTH


def _lane_parts(l):
    quarter, i = divmod(l, 32)
    return quarter % 2, quarter // 2, i


def _q_perm():
    idx = np.zeros(Q_WIDTH, np.int32)
    for c in range(N_GROUP):
        for l in range(LANES):
            slot, parity, i = _lane_parts(l)
            idx[c * LANES + l] = (c + N_GROUP * slot) * HEAD_DIM + 2 * i + parity
    return idx


def _k_perm():
    idx = np.zeros(KV_WIDTH, np.int32)
    for l in range(LANES):
        slot, parity, i = _lane_parts(l)
        idx[l] = slot * HEAD_DIM + 2 * i + parity
    return idx


def _gain_perm():
    return np.array([2 * _lane_parts(l)[2] + _lane_parts(l)[1] for l in range(LANES)], np.int32)


def _out_row_perm():
    idx = np.zeros(Q_WIDTH, np.int32)
    for c in range(N_GROUP):
        for half in range(2):
            for d in range(HEAD_DIM):
                idx[c * LANES + half * HEAD_DIM + d] = (c + N_GROUP * half) * HEAD_DIM + d
    return idx


_QP, _KP, _GP, _OP = _q_perm(), _k_perm(), _gain_perm(), _out_row_perm()
_PROJ_COLS = np.concatenate([
    _QA + _QP, _QC + _QP, _KA + _KP, _KC + _KP,
    np.arange(_VA, _VA + KV_WIDTH), np.arange(_VC, _VC + KV_WIDTH), np.arange(_U, _U + POOL_WIDTH)])
_ZQA, _ZQC, _ZKA, _ZKC, _ZVA, _ZVC, _ZU = 0, 384, 768, 896, 1024, 1152, 1280


def _const_spec(shape):
    nd = len(shape)
    return pl.BlockSpec(shape, lambda *_: (0,) * nd, pipeline_mode=pl.Buffered(1))


def _params():
    return pltpu.CompilerParams(dimension_semantics=("arbitrary", "arbitrary"),
                                vmem_limit_bytes=VMEM_LIMIT_BYTES)


def _modulate(x, gain_scale, shift):
    ms = jnp.mean(x * x, axis=-1, keepdims=True)
    return x * lax.rsqrt(ms + EPS) * gain_scale + shift


def _dot(a, b):
    return jnp.dot(a, b, preferred_element_type=F32)


def _adaln_kernel(v_ref, w_ref, b_ref, o_ref):
    v = v_ref[...]
    s = (v * jax.nn.sigmoid(v)).astype(BF16)
    o_ref[0] = _dot(s, w_ref[0].astype(BF16)) + b_ref[0]


def _adaln(v, w_ada, b_ada):
    rows = v.shape[0]
    tn = 1536
    return pl.pallas_call(
        _adaln_kernel,
        grid=(DEPTH, 6 * D_MODEL // tn),
        in_specs=[pl.BlockSpec((rows, D_MODEL), lambda l, j: (0, 0)),
                  pl.BlockSpec((1, D_MODEL, tn), lambda l, j: (l, 0, j)),
                  pl.BlockSpec((1, 1, tn), lambda l, j: (l, 0, j))],
        out_specs=pl.BlockSpec((1, rows, tn), lambda l, j: (l, 0, j)),
        out_shape=jax.ShapeDtypeStruct((DEPTH, rows, 6 * D_MODEL), F32),
        compiler_params=_params(),
        name="adaln",
    )(v, w_ada, b_ada.reshape(DEPTH, 1, 6 * D_MODEL))


def _norm_rope(zc, gain, cos, sin, slot0):
    sq = zc * zc
    s0 = jnp.sum(jnp.where(slot0, sq, 0.0), axis=-1, keepdims=True)
    s1 = jnp.sum(jnp.where(slot0, 0.0, sq), axis=-1, keepdims=True)
    r = jnp.where(slot0, lax.rsqrt(s0 * (1.0 / HEAD_DIM) + EPS), lax.rsqrt(s1 * (1.0 / HEAD_DIM) + EPS))
    y = zc * r * gain
    return y * cos + pltpu.roll(y, LANES // 2, 1) * sin


def _proj_kernel(x_ref, xp_ref, xn_ref, mod_ref, g_ref, w_ref, hg_ref, cos_ref, sin_ref, wp_ref, ps_ref,
                 qa_ref, qc_ref, ka_ref, kc_ref, va_ref, vc_ref, ob_ref, *, tm, seq_len):
    i = pl.program_id(1)
    mod = mod_ref[0]
    shift = mod[0:1]
    gain_scale = g_ref[...] * (1.0 + mod[1:2])
    w = w_ref[...]
    h = _modulate(x_ref[0], gain_scale, shift).astype(BF16)
    z = _dot(h, w)

    lane = lax.broadcasted_iota(jnp.int32, (1, LANES), 1)
    slot0 = ((lane >> 5) & 1) == 0
    cos = cos_ref[...]
    sin = sin_ref[...]
    hg = hg_ref[...]
    for c in range(N_GROUP):
        lo = c * LANES
        qa_ref[0, :, lo:lo + LANES] = _norm_rope(
            z[:, _ZQA + lo:_ZQA + lo + LANES], hg[0:1], cos, sin, slot0).astype(BF16)
        qc_ref[0, :, lo:lo + LANES] = _norm_rope(
            z[:, _ZQC + lo:_ZQC + lo + LANES], hg[2:3], cos, sin, slot0).astype(BF16)
    ka_ref[0] = _norm_rope(z[:, _ZKA:_ZKA + LANES], hg[1:2], cos, sin, slot0).astype(BF16)
    kc_ref[0] = _norm_rope(z[:, _ZKC:_ZKC + LANES], hg[3:4], cos, sin, slot0).astype(BF16)
    va_ref[0] = z[:, _ZVA:_ZVA + LANES].astype(BF16)
    vc_ref[0] = z[:, _ZVC:_ZVC + LANES].astype(BF16)

    wu = w[:, _ZU:_ZU + POOL_WIDTH]

    def halo(ref, valid):
        hh = _modulate(ref[0], gain_scale, shift).astype(BF16)
        return jnp.where(valid, _dot(hh, wu), 0.0)

    u_prev = halo(xp_ref, i > 0)[HALO_ROWS - POOL_HALO:]
    u_next = halo(xn_ref, i < pl.num_programs(1) - 1)[:POOL_HALO]
    u = z[:, _ZU:_ZU + POOL_WIDTH]
    ue = jnp.concatenate([u_prev, u, u_next], axis=0)
    n0 = tm + 2 * POOL_HALO
    fwd = lambda a, k: pltpu.roll(a, n0 - k, 0)
    w2 = ue + fwd(ue, 1)
    w4 = w2 + fwd(w2, 2)
    w8 = w4 + fwd(w4, 4)
    w16 = w8 + fwd(w8, 8)
    lane_p = lax.broadcasted_iota(jnp.int32, (1, POOL_WIDTH), 1)
    grp = lane_p >> 6
    centred = jnp.where(grp == 0, pltpu.roll(w2, 1, 0),
                        jnp.where(grp == 1, pltpu.roll(w4, 2, 0),
                                  jnp.where(grp == 2, pltpu.roll(w8, 4, 0), pltpu.roll(w16, 8, 0))))
    tot = centred[POOL_HALO:POOL_HALO + tm]
    half = jnp.left_shift(1, grp)
    t = i * tm + lax.broadcasted_iota(jnp.int32, (tm, 1), 0)
    cnt = (jnp.minimum(t + half, seq_len) - jnp.maximum(t - half, 0)).astype(F32)
    pooled = tot / cnt - u
    ob_ref[0] = (_dot(pooled.astype(BF16), wp_ref[...]) * ps_ref[...]).astype(BF16)


def _project(x, mod, gain, w_proj, head_gains, cos_tab, sin_tab, w_pool_bd, pool_scale, *, tm):
    b, seq_len, _ = x.shape
    nb = seq_len // tm
    hb = tm // HALO_ROWS
    mod_map = (lambda bi, i: (bi, 0, 0)) if mod.shape[0] == b else (lambda bi, i: (0, 0, 0))
    tok = lambda width: pl.BlockSpec((1, tm, width), lambda bi, i: (bi, i, 0))
    out = lambda width: jax.ShapeDtypeStruct((b, seq_len, width), BF16)
    return pl.pallas_call(
        functools.partial(_proj_kernel, tm=tm, seq_len=seq_len),
        grid=(b, nb),
        in_specs=[tok(D_MODEL),
                  pl.BlockSpec((1, HALO_ROWS, D_MODEL), lambda bi, i: (bi, jnp.maximum(i * hb - 1, 0), 0)),
                  pl.BlockSpec((1, HALO_ROWS, D_MODEL),
                               lambda bi, i: (bi, jnp.minimum((i + 1) * hb, nb * hb - 1), 0)),
                  pl.BlockSpec((1, 6, D_MODEL), mod_map),
                  _const_spec((1, D_MODEL)),
                  _const_spec((D_MODEL, PROJ_WIDTH)),
                  _const_spec((8, LANES)),
                  pl.BlockSpec((tm, LANES), lambda bi, i: (i, 0)),
                  pl.BlockSpec((tm, LANES), lambda bi, i: (i, 0)),
                  _const_spec((POOL_WIDTH, POOL_WIDTH)),
                  _const_spec((1, POOL_WIDTH))],
        out_specs=[tok(Q_WIDTH), tok(Q_WIDTH), tok(KV_WIDTH), tok(KV_WIDTH), tok(KV_WIDTH), tok(KV_WIDTH),
                   tok(POOL_WIDTH)],
        out_shape=[out(Q_WIDTH), out(Q_WIDTH), out(KV_WIDTH), out(KV_WIDTH), out(KV_WIDTH), out(KV_WIDTH),
                   out(POOL_WIDTH)],
        compiler_params=_params(),
        name="project",
    )(x, x, x, mod, gain, w_proj, head_gains, cos_tab, sin_tab, w_pool_bd, pool_scale)


_NT = (((1,), (1,)), ((), ()))


def _attn_kernel(*refs, tq, has_lat, window, has_sink, seq_len):
    refs = list(refs)
    sink_ref = refs.pop(0) if has_sink else None
    q_ref, kc_ref, vc_ref = refs[:3]
    if has_lat:
        kl_ref, vl_ref = refs[3:5]
    o_ref = refs[-1]
    i = pl.program_id(1)

    lane = lax.broadcasted_iota(jnp.int32, (1, LANES), 1)
    slot0 = ((lane >> 5) & 1) == 0
    q = q_ref[0]
    zero = jnp.zeros((), BF16)
    q6 = jnp.concatenate(
        [jnp.where(slot0 if kv == 0 else ~slot0, q[:, g * LANES:(g + 1) * LANES], zero)
         for kv in range(N_KV) for g in range(N_GROUP)], axis=0)

    s_ctx = lax.dot_general(q6, kc_ref[0], _NT, preferred_element_type=F32)
    m = jnp.max(s_ctx, axis=-1, keepdims=True)
    if has_lat:
        if window:
            nk = tq + 2 * WINDOW
            start = pl.multiple_of(jnp.clip((i - 1) * tq, 0, seq_len - nk), tq)
            kl = kl_ref[0, pl.ds(start, nk), :]
            vl = vl_ref[0, pl.ds(start, nk), :]
            s_lat = lax.dot_general(q6, kl, _NT, preferred_element_type=F32)
            r = lax.broadcasted_iota(jnp.int32, (tq, nk), 0)
            c = lax.broadcasted_iota(jnp.int32, (tq, nk), 1)
            valid = jnp.abs(i * tq - start + r - c) <= WINDOW
            s_lat = jnp.where(valid[None], s_lat.reshape(N_HEADS, tq, nk), NEG).reshape(N_HEADS * tq, nk)
        else:
            kl = kl_ref[0]
            vl = vl_ref[0]
            s_lat = lax.dot_general(q6, kl, _NT, preferred_element_type=F32)
        m = jnp.maximum(m, jnp.max(s_lat, axis=-1, keepdims=True))
    if has_sink:
        sink = jnp.concatenate([jnp.full((tq, 1), sink_ref[j], F32) for j in range(N_HEADS)], axis=0)
        m = jnp.maximum(m, sink)

    p_ctx = jnp.exp(s_ctx - m)
    denom = jnp.sum(p_ctx, axis=-1, keepdims=True)
    o = _dot(p_ctx.astype(BF16), vc_ref[0])
    if has_lat:
        p_lat = jnp.exp(s_lat - m)
        denom = denom + jnp.sum(p_lat, axis=-1, keepdims=True)
        o = o + _dot(p_lat.astype(BF16), vl)
    if has_sink:
        denom = denom + jnp.exp(sink - m)
    o = o / denom
    for g in range(N_GROUP):
        blk = jnp.where(lane < HEAD_DIM, o[g * tq:(g + 1) * tq], o[(N_GROUP + g) * tq:(N_GROUP + g + 1) * tq])
        o_ref[0, :, g * LANES:(g + 1) * LANES] = blk.astype(BF16)


def _attention(q, k_ctx, v_ctx, k_lat=None, v_lat=None, sink=None, *, window=False, tq=128):
    b, nq, _ = q.shape
    n_ctx = k_ctx.shape[1]
    has_lat = k_lat is not None
    has_sink = sink is not None
    args, specs = [], []
    if has_sink:
        args.append(sink.astype(F32))
        specs.append(pl.BlockSpec(memory_space=pltpu.SMEM))
    args += [q, k_ctx, v_ctx]
    specs += [pl.BlockSpec((1, tq, Q_WIDTH), lambda bi, i: (bi, i, 0)),
              pl.BlockSpec((1, n_ctx, KV_WIDTH), lambda bi, i: (bi, 0, 0)),
              pl.BlockSpec((1, n_ctx, KV_WIDTH), lambda bi, i: (bi, 0, 0))]
    if has_lat:
        n_lat = k_lat.shape[1]
        args += [k_lat, v_lat]
        specs += [pl.BlockSpec((1, n_lat, KV_WIDTH), lambda bi, i: (bi, 0, 0)),
                  pl.BlockSpec((1, n_lat, KV_WIDTH), lambda bi, i: (bi, 0, 0))]
    return pl.pallas_call(
        functools.partial(_attn_kernel, tq=tq, has_lat=has_lat, window=window, has_sink=has_sink, seq_len=nq),
        grid=(b, nq // tq),
        in_specs=specs,
        out_specs=pl.BlockSpec((1, tq, Q_WIDTH), lambda bi, i: (bi, i, 0)),
        out_shape=jax.ShapeDtypeStruct((b, nq, Q_WIDTH), BF16),
        compiler_params=_params(),
        name="window_attn" if window else ("global_attn" if has_lat else "ctx_attn"),
    )(*args)


def _merge_kernel(x_ref, mod_ref, g_ref, oa_ref, ob_ref, oc_ref, wg_ref, wa_ref, wb_ref, wc_ref, wo_ref, out_ref):
    mod = mod_ref[0]
    x = x_ref[0]
    h = _modulate(x, g_ref[...] * (1.0 + mod[1:2]), mod[0:1]).astype(BF16)
    y = None
    for k, (o_ref, w_ref) in enumerate(((oa_ref, wa_ref), (ob_ref, wb_ref), (oc_ref, wc_ref))):
        gate = jax.nn.sigmoid(_dot(h, wg_ref[:, k * D_MODEL:(k + 1) * D_MODEL]))
        term = gate * _dot(o_ref[0], w_ref[...])
        y = term if y is None else y + term
    out_ref[0] = x + mod[2:3] * _dot(y.astype(BF16), wo_ref[...])


def _merge(x, mod, gain, oa, ob, oc, w_gates, w_a, w_b, w_c, w_out, *, tm):
    b, seq_len, _ = x.shape
    mod_map = (lambda bi, i: (bi, 0, 0)) if mod.shape[0] == b else (lambda bi, i: (0, 0, 0))
    tok = lambda width: pl.BlockSpec((1, tm, width), lambda bi, i: (bi, i, 0))
    return pl.pallas_call(
        _merge_kernel,
        grid=(b, seq_len // tm),
        in_specs=[tok(D_MODEL), pl.BlockSpec((1, 6, D_MODEL), mod_map), _const_spec((1, D_MODEL)),
                  tok(Q_WIDTH), tok(POOL_WIDTH), tok(Q_WIDTH),
                  _const_spec((D_MODEL, 3 * D_MODEL)), _const_spec((Q_WIDTH, D_MODEL)),
                  _const_spec((POOL_WIDTH, D_MODEL)), _const_spec((Q_WIDTH, D_MODEL)),
                  _const_spec((D_MODEL, D_MODEL))],
        out_specs=tok(D_MODEL),
        out_shape=jax.ShapeDtypeStruct(x.shape, F32),
        compiler_params=_params(),
        name="merge",
    )(x, mod, gain, oa, ob, oc, w_gates, w_a, w_b, w_c, w_out)


FF_CHUNK = 1024


def _mlp_kernel(x_ref, mod_ref, g_ref, w1_ref, w2_ref, out_ref):
    mod = mod_ref[0]
    x = x_ref[0]
    h = _modulate(x, g_ref[...] * (1.0 + mod[4:5]), mod[3:4]).astype(BF16)
    acc = None
    for k in range(D_FF // FF_CHUNK):
        a = jnp.maximum(_dot(h, w1_ref[:, k * FF_CHUNK:(k + 1) * FF_CHUNK]), 0.0)
        part = _dot((a * a).astype(BF16), w2_ref[k * FF_CHUNK:(k + 1) * FF_CHUNK, :])
        acc = part if acc is None else acc + part
    out_ref[0] = x + mod[5:6] * acc


def _mlp(x, mod, gain, w1, w2, *, tm):
    b, seq_len, _ = x.shape
    mod_map = (lambda bi, i: (bi, 0, 0)) if mod.shape[0] == b else (lambda bi, i: (0, 0, 0))
    tok = pl.BlockSpec((1, tm, D_MODEL), lambda bi, i: (bi, i, 0))
    return pl.pallas_call(
        _mlp_kernel,
        grid=(b, seq_len // tm),
        in_specs=[tok, pl.BlockSpec((1, 6, D_MODEL), mod_map), _const_spec((1, D_MODEL)),
                  _const_spec((D_MODEL, D_FF)), _const_spec((D_FF, D_MODEL))],
        out_specs=tok,
        out_shape=jax.ShapeDtypeStruct(x.shape, F32),
        compiler_params=_params(),
        name="mlp",
    )(x, mod, gain, w1, w2)


def _rope_tables(n_tok):
    rows = n_tok // GRID_W
    r = jnp.repeat(jnp.arange(rows, dtype=F32), GRID_W)
    col = jnp.tile(jnp.arange(GRID_W, dtype=F32), rows)
    inv = 1.0 / (ROPE_THETA ** (jnp.arange(0, AXIS_DIM, 2, dtype=F32) / AXIS_DIM))
    ang = jnp.concatenate([r[:, None] * inv, col[:, None] * inv], axis=-1)
    pair = np.arange(LANES) % 32
    sign = np.where(np.arange(LANES) < LANES // 2, -1.0, 1.0).astype(np.float32)
    return jnp.cos(ang)[:, pair], jnp.sin(ang)[:, pair] * sign


def kernel(x, c, ctx, c_ctx, w_ada, b_ada, norm1, norm2, w_in, q_norm_a, k_norm_a, q_norm_c, k_norm_c, sink_c,
           w_pool, pool_scale, w_br_a, w_br_b, w_br_c, w_out, w_mlp1, w_mlp2):
    batch, seq_len, _ = x.shape
    n_ctx = ctx.shape[1]
    pad = (-(batch + 1)) % 8
    v = jnp.concatenate([c, c_ctx[None], jnp.zeros((pad, D_MODEL), F32)], axis=0)
    mods = _adaln(v, w_ada, b_ada)

    cos_lat, sin_lat = _rope_tables(seq_len)
    cos_ctx = jnp.ones((n_ctx, LANES), F32)
    sin_ctx = jnp.zeros((n_ctx, LANES), F32)
    q_scale = HEAD_DIM ** -0.5

    xc = ctx
    for l in range(DEPTH):
        mod_lat = mods[l, :batch].reshape(batch, 6, D_MODEL)
        mod_ctx = mods[l, batch:batch + 1].reshape(1, 6, D_MODEL)
        w_proj = w_in[l][:, _PROJ_COLS].astype(BF16)
        w_gates = w_in[l][:, _GATES:].astype(BF16)
        head_gains = jnp.concatenate([
            (q_norm_a[l] * q_scale)[_GP][None], k_norm_a[l][_GP][None],
            (q_norm_c[l] * q_scale)[_GP][None], k_norm_c[l][_GP][None],
            jnp.zeros((4, LANES), F32)], axis=0)
        w_pool_bd = jax.scipy.linalg.block_diag(*[w_pool[l, g] for g in range(w_pool.shape[1])]).astype(BF16)
        ps = pool_scale[l][None]
        g1 = norm1[l][None]
        g2 = norm2[l][None]
        wa = w_br_a[l][_OP].astype(BF16)
        wb = w_br_b[l].astype(BF16)
        wc = w_br_c[l][_OP].astype(BF16)
        wo = w_out[l].astype(BF16)
        w1 = w_mlp1[l].astype(BF16)
        w2 = w_mlp2[l].astype(BF16)

        cqa, cqc, cka, ckc, cva, cvc, cob = _project(
            xc, mod_ctx, g1, w_proj, head_gains, cos_ctx, sin_ctx, w_pool_bd, ps, tm=n_ctx)
        qa, qc, ka, kc, va, vc, ob = _project(
            x, mod_lat, g1, w_proj, head_gains, cos_lat, sin_lat, w_pool_bd, ps, tm=512)

        oa = _attention(qa, cka, cva, ka, va)
        oc = _attention(qc, ckc, cvc, kc, vc, sink_c[l], window=True)
        x = _merge(x, mod_lat, g1, oa, ob, oc, w_gates, wa, wb, wc, wo, tm=512)
        x = _mlp(x, mod_lat, g2, w1, w2, tm=512)

        if l < DEPTH - 1:
            coa = _attention(cqa, cka, cva)
            coc = _attention(cqc, ckc, cvc, sink=sink_c[l])
            xc = _merge(xc, mod_ctx, g1, coa, cob, coc, w_gates, wa, wb, wc, wo, tm=n_ctx)
            xc = _mlp(xc, mod_ctx, g2, w1, w2, tm=n_ctx)
    return x
```

```python
import functools

import numpy as np
import jax
import jax.numpy as jnp
from jax import lax
from jax.experimental import pallas as pl
from jax.experimental.pallas import tpu as pltpu

D_MODEL = 1024
DEPTH = 2
GRID_W = 64
HEAD_DIM = 64
AXIS_DIM = HEAD_DIM // 2
ROPE_THETA = 10000.0
N_HEADS = 6
N_KV = 2
N_GROUP = N_HEADS // N_KV
Q_WIDTH = N_HEADS * HEAD_DIM
KV_WIDTH = N_KV * HEAD_DIM
POOL_WIDTH = 256
POOL_CH = 64
POOL_HALO = 8
WINDOW = 128
D_FF = 4 * D_MODEL
EPS = 1e-6
NEG = -1e30
LOG2E = 1.4426950408889634

LANES = 128
HALO_ROWS = 16
VMEM_LIMIT_BYTES = 56 * 1024 * 1024

F32 = jnp.float32
BF16 = jnp.bfloat16

_QA, _KA, _VA, _QC, _KC, _VC, _U, _GATES = 0, 384, 512, 640, 1024, 1152, 1280, 1536
PROJ_WIDTH = 2 * Q_WIDTH + 4 * KV_WIDTH + POOL_WIDTH


def _lane_parts(l):
    quarter, i = divmod(l, 32)
    return quarter % 2, quarter // 2, i


def _q_perm():
    idx = np.zeros(Q_WIDTH, np.int32)
    for c in range(N_GROUP):
        for l in range(LANES):
            slot, parity, i = _lane_parts(l)
            idx[c * LANES + l] = (c + N_GROUP * slot) * HEAD_DIM + 2 * i + parity
    return idx


def _k_perm():
    idx = np.zeros(KV_WIDTH, np.int32)
    for l in range(LANES):
        slot, parity, i = _lane_parts(l)
        idx[l] = slot * HEAD_DIM + 2 * i + parity
    return idx


def _gain_perm():
    return np.array([2 * _lane_parts(l)[2] + _lane_parts(l)[1] for l in range(LANES)], np.int32)


def _out_row_perm():
    idx = np.zeros(Q_WIDTH, np.int32)
    for c in range(N_GROUP):
        for half in range(2):
            for d in range(HEAD_DIM):
                idx[c * LANES + half * HEAD_DIM + d] = (c + N_GROUP * half) * HEAD_DIM + d
    return idx


_QP, _KP, _GP, _OP = _q_perm(), _k_perm(), _gain_perm(), _out_row_perm()
_PROJ_COLS = np.concatenate([
    _QA + _QP, _QC + _QP, _KA + _KP, _KC + _KP,
    np.arange(_VA, _VA + KV_WIDTH), np.arange(_VC, _VC + KV_WIDTH), np.arange(_U, _U + POOL_WIDTH)])
_HEAD_MEAN = np.array([[(a // LANES == b // LANES and _lane_parts(a % LANES)[0] == _lane_parts(b % LANES)[0])
                        / HEAD_DIM for b in range(2 * LANES)] for a in range(2 * LANES)], np.float32)
_ZQA, _ZQC, _ZKA, _ZKC, _ZVA, _ZVC, _ZU = 0, 384, 768, 896, 1024, 1152, 1280


def _const_spec(shape):
    nd = len(shape)
    return pl.BlockSpec(shape, lambda *_: (0,) * nd, pipeline_mode=pl.Buffered(1))


def _params():
    return pltpu.CompilerParams(dimension_semantics=("arbitrary", "arbitrary"),
                                vmem_limit_bytes=VMEM_LIMIT_BYTES)


def _modulate(x, gain_scale, shift):
    ms = jnp.mean(x * x, axis=-1, keepdims=True)
    return x * lax.rsqrt(ms + EPS) * gain_scale + shift


def _dot(a, b):
    return jnp.dot(a, b, preferred_element_type=F32)


def _adaln_kernel(v_ref, w_ref, b_ref, o_ref):
    v = v_ref[...]
    s = (v * jax.nn.sigmoid(v)).astype(BF16)
    o_ref[0] = _dot(s, w_ref[0].astype(BF16)) + b_ref[0]


def _adaln(v, w_ada, b_ada):
    rows = v.shape[0]
    tn = 1536
    return pl.pallas_call(
        _adaln_kernel,
        grid=(DEPTH, 6 * D_MODEL // tn),
        in_specs=[pl.BlockSpec((rows, D_MODEL), lambda l, j: (0, 0)),
                  pl.BlockSpec((1, D_MODEL, tn), lambda l, j: (l, 0, j)),
                  pl.BlockSpec((1, 1, tn), lambda l, j: (l, 0, j))],
        out_specs=pl.BlockSpec((1, rows, tn), lambda l, j: (l, 0, j)),
        out_shape=jax.ShapeDtypeStruct((DEPTH, rows, 6 * D_MODEL), F32),
        compiler_params=_params(),
        name="adaln",
    )(v, w_ada, b_ada.reshape(DEPTH, 1, 6 * D_MODEL))


def _proj_kernel(x_ref, xp_ref, xn_ref, mod_ref, g_ref, w_ref, hg_ref, hm_ref, cos_ref, sin_ref, wp_ref, ps_ref,
                 qa_ref, qc_ref, ka_ref, kc_ref, va_ref, vc_ref, ob_ref, a0_ref, a1_ref, a2_ref, a3_ref,
                 *, tm, seq_len):
    i = pl.program_id(1)
    mod = mod_ref[0]
    shift = mod[0:1]
    gain_scale = g_ref[...] * (1.0 + mod[1:2])
    w = w_ref[...]
    h = _modulate(x_ref[0], gain_scale, shift).astype(BF16)
    z = _dot(h, w)

    cos = cos_ref[...]
    sin = sin_ref[...]
    hm = hm_ref[...]
    outs = ([(qa_ref, c) for c in range(N_GROUP)] + [(qc_ref, c) for c in range(N_GROUP)]
            + [(ka_ref, 0), (kc_ref, 0)])
    for p in range(4):
        zp = z[:, 2 * LANES * p:2 * LANES * (p + 1)]
        sq = zp * zp
        hi = sq.astype(BF16)
        lo = (sq - hi.astype(F32)).astype(BF16)
        ms = _dot(hi, hm) + _dot(lo, hm)
        y = zp * lax.rsqrt(ms + EPS) * hg_ref[:, 2 * LANES * p:2 * LANES * (p + 1)]
        for half in range(2):
            yc = y[:, half * LANES:(half + 1) * LANES]
            ref, c = outs[2 * p + half]
            ref[0, :, c * LANES:(c + 1) * LANES] = (yc * cos + pltpu.roll(yc, LANES // 2, 1) * sin).astype(BF16)
    va_ref[0] = z[:, _ZVA:_ZVA + LANES].astype(BF16)
    vc_ref[0] = z[:, _ZVC:_ZVC + LANES].astype(BF16)

    wu = w[:, _ZU:_ZU + POOL_WIDTH]

    def halo(ref, valid):
        hh = _modulate(ref[0], gain_scale, shift).astype(BF16)
        return jnp.where(valid, _dot(hh, wu), 0.0)

    u = z[:, _ZU:_ZU + POOL_WIDTH]
    n0 = tm + 2 * POOL_HALO

    def put(ref, val):
        ref[0:n0] = val
        ref[n0:n0 + POOL_HALO] = jnp.zeros((POOL_HALO, val.shape[1]), F32)

    a0_ref[0:POOL_HALO] = halo(xp_ref, i > 0)[HALO_ROWS - POOL_HALO:]
    a0_ref[POOL_HALO:POOL_HALO + tm] = u
    a0_ref[POOL_HALO + tm:n0] = halo(xn_ref, i < pl.num_programs(1) - 1)[:POOL_HALO]
    a0_ref[n0:n0 + POOL_HALO] = jnp.zeros((POOL_HALO, POOL_WIDTH), F32)
    w2 = a0_ref[0:n0] + a0_ref[1:n0 + 1]
    put(a1_ref, w2)
    w4 = w2 + a1_ref[2:n0 + 2]
    put(a2_ref, w4)
    w8 = w4[:, LANES:] + a2_ref[4:n0 + 4, LANES:]
    put(a3_ref, w8)
    w16 = w8 + a3_ref[8:n0 + 8]
    lane = lax.broadcasted_iota(jnp.int32, (1, LANES), 1)
    first = lane < POOL_CH
    t = i * tm + lax.broadcasted_iota(jnp.int32, (tm, 1), 0)
    pooled = []
    for col, (narrow, wide) in enumerate(((a1_ref[7:7 + tm, :LANES], a2_ref[6:6 + tm, :LANES]),
                                          (a3_ref[4:4 + tm], w16[0:tm]))):
        half = jnp.where(first, 1, 2) << (2 * col)
        cnt = (jnp.minimum(t + half, seq_len) - jnp.maximum(t - half, 0)).astype(F32)
        pooled.append(jnp.where(first, narrow, wide) / cnt - u[:, col * LANES:(col + 1) * LANES])
    pooled = jnp.concatenate(pooled, axis=1)
    ob_ref[0] = (_dot(pooled.astype(BF16), wp_ref[...]) * ps_ref[...]).astype(BF16)


def _project(x, mod, gain, w_proj, head_gains, cos_tab, sin_tab, w_pool_bd, pool_scale, *, tm):
    b, seq_len, _ = x.shape
    nb = seq_len // tm
    hb = tm // HALO_ROWS
    mod_map = (lambda bi, i: (bi, 0, 0)) if mod.shape[0] == b else (lambda bi, i: (0, 0, 0))
    tok = lambda width: pl.BlockSpec((1, tm, width), lambda bi, i: (bi, i, 0))
    out = lambda width: jax.ShapeDtypeStruct((b, seq_len, width), BF16)
    return pl.pallas_call(
        functools.partial(_proj_kernel, tm=tm, seq_len=seq_len),
        grid=(b, nb),
        in_specs=[tok(D_MODEL),
                  pl.BlockSpec((1, HALO_ROWS, D_MODEL), lambda bi, i: (bi, jnp.maximum(i * hb - 1, 0), 0)),
                  pl.BlockSpec((1, HALO_ROWS, D_MODEL),
                               lambda bi, i: (bi, jnp.minimum((i + 1) * hb, nb * hb - 1), 0)),
                  pl.BlockSpec((1, 6, D_MODEL), mod_map),
                  _const_spec((1, D_MODEL)),
                  _const_spec((D_MODEL, PROJ_WIDTH)),
                  _const_spec((1, 8 * LANES)),
                  _const_spec((2 * LANES, 2 * LANES)),
                  pl.BlockSpec((tm, LANES), lambda bi, i: (i, 0)),
                  pl.BlockSpec((tm, LANES), lambda bi, i: (i, 0)),
                  _const_spec((POOL_WIDTH, POOL_WIDTH)),
                  _const_spec((1, POOL_WIDTH))],
        out_specs=[tok(Q_WIDTH), tok(Q_WIDTH), tok(KV_WIDTH), tok(KV_WIDTH), tok(KV_WIDTH), tok(KV_WIDTH),
                   tok(POOL_WIDTH)],
        out_shape=[out(Q_WIDTH), out(Q_WIDTH), out(KV_WIDTH), out(KV_WIDTH), out(KV_WIDTH), out(KV_WIDTH),
                   out(POOL_WIDTH)],
        scratch_shapes=[pltpu.VMEM((tm + 3 * POOL_HALO, POOL_WIDTH), F32)] * 3
                       + [pltpu.VMEM((tm + 3 * POOL_HALO, LANES), F32)],
        compiler_params=_params(),
        name="project",
    )(x, x, x, mod, gain, w_proj, head_gains, jnp.asarray(_HEAD_MEAN, BF16), cos_tab, sin_tab, w_pool_bd,
      pool_scale)


_NT = (((1,), (1,)), ((), ()))


def _attn_kernel(*refs, tq, has_lat, window, has_sink, seq_len):
    refs = list(refs)
    sink_ref = refs.pop(0) if has_sink else None
    q_ref, kc_ref, vc_ref = refs[:3]
    if has_lat:
        kl_ref, vl_ref = refs[3:5]
    o_ref = refs[-1]
    i = pl.program_id(1)

    lane = lax.broadcasted_iota(jnp.int32, (1, LANES), 1)
    slot0 = ((lane >> 5) & 1) == 0
    zero = jnp.zeros((), BF16)
    kc = kc_ref[0]
    vc = vc_ref[0]
    if has_lat:
        if window:
            nk = tq + 2 * WINDOW
            start = pl.multiple_of(jnp.clip(i * tq - WINDOW, 0, seq_len - nk), WINDOW)
            kl = kl_ref[0, pl.ds(start, nk), :]
            vl = vl_ref[0, pl.ds(start, nk), :]
            r = lax.broadcasted_iota(jnp.int32, (tq, nk), 0)
            c = lax.broadcasted_iota(jnp.int32, (tq, nk), 1)
            valid = jnp.abs(i * tq - start + r - c) <= WINDOW
        else:
            kl = kl_ref[0]
            vl = vl_ref[0]

    outs = []
    for j in range(N_HEADS):
        kv, g = divmod(j, N_GROUP)
        qj = jnp.where(slot0 if kv == 0 else ~slot0, q_ref[0, :, g * LANES:(g + 1) * LANES], zero)
        s_ctx = lax.dot_general(qj, kc, _NT, preferred_element_type=F32)
        m = jnp.max(s_ctx, axis=-1, keepdims=True)
        if has_lat:
            s_lat = lax.dot_general(qj, kl, _NT, preferred_element_type=F32)
            if window:
                s_lat = jnp.where(valid, s_lat, NEG)
            m = jnp.maximum(m, jnp.max(s_lat, axis=-1, keepdims=True))
        if has_sink:
            sink = sink_ref[j] * LOG2E
            m = jnp.maximum(m, sink)
        p_ctx = jnp.exp2(s_ctx - m)
        denom = jnp.sum(p_ctx, axis=-1, keepdims=True)
        o = _dot(p_ctx.astype(BF16), vc)
        if has_lat:
            p_lat = jnp.exp2(s_lat - m)
            denom = denom + jnp.sum(p_lat, axis=-1, keepdims=True)
            o = o + _dot(p_lat.astype(BF16), vl)
        if has_sink:
            denom = denom + jnp.exp2(sink - m)
        outs.append(o / denom)
    for g in range(N_GROUP):
        blk = jnp.where(lane < HEAD_DIM, outs[g], outs[N_GROUP + g])
        o_ref[0, :, g * LANES:(g + 1) * LANES] = blk.astype(BF16)


def _attention(q, k_ctx, v_ctx, k_lat=None, v_lat=None, sink=None, *, window=False, tq=256):
    b, nq, _ = q.shape
    n_ctx = k_ctx.shape[1]
    has_lat = k_lat is not None
    has_sink = sink is not None
    args, specs = [], []
    if has_sink:
        args.append(sink.astype(F32))
        specs.append(pl.BlockSpec(memory_space=pltpu.SMEM))
    args += [q, k_ctx, v_ctx]
    specs += [pl.BlockSpec((1, tq, Q_WIDTH), lambda bi, i: (bi, i, 0)),
              pl.BlockSpec((1, n_ctx, KV_WIDTH), lambda bi, i: (bi, 0, 0)),
              pl.BlockSpec((1, n_ctx, KV_WIDTH), lambda bi, i: (bi, 0, 0))]
    if has_lat:
        n_lat = k_lat.shape[1]
        args += [k_lat, v_lat]
        specs += [pl.BlockSpec((1, n_lat, KV_WIDTH), lambda bi, i: (bi, 0, 0)),
                  pl.BlockSpec((1, n_lat, KV_WIDTH), lambda bi, i: (bi, 0, 0))]
    return pl.pallas_call(
        functools.partial(_attn_kernel, tq=tq, has_lat=has_lat, window=window, has_sink=has_sink, seq_len=nq),
        grid=(b, nq // tq),
        in_specs=specs,
        out_specs=pl.BlockSpec((1, tq, Q_WIDTH), lambda bi, i: (bi, i, 0)),
        out_shape=jax.ShapeDtypeStruct((b, nq, Q_WIDTH), BF16),
        compiler_params=_params(),
        name="window_attn" if window else ("global_attn" if has_lat else "ctx_attn"),
    )(*args)


def _merge_kernel(x_ref, mod_ref, g_ref, oa_ref, ob_ref, oc_ref, wg_ref, wa_ref, wb_ref, wc_ref, wo_ref, out_ref):
    mod = mod_ref[0]
    x = x_ref[0]
    h = _modulate(x, g_ref[...] * (1.0 + mod[1:2]), mod[0:1]).astype(BF16)
    y = None
    for k, (o_ref, w_ref) in enumerate(((oa_ref, wa_ref), (ob_ref, wb_ref), (oc_ref, wc_ref))):
        gate = jax.nn.sigmoid(_dot(h, wg_ref[:, k * D_MODEL:(k + 1) * D_MODEL]))
        term = gate * _dot(o_ref[0], w_ref[...])
        y = term if y is None else y + term
    out_ref[0] = x + mod[2:3] * _dot(y.astype(BF16), wo_ref[...])


def _merge(x, mod, gain, oa, ob, oc, w_gates, w_a, w_b, w_c, w_out, *, tm):
    b, seq_len, _ = x.shape
    mod_map = (lambda bi, i: (bi, 0, 0)) if mod.shape[0] == b else (lambda bi, i: (0, 0, 0))
    tok = lambda width: pl.BlockSpec((1, tm, width), lambda bi, i: (bi, i, 0))
    return pl.pallas_call(
        _merge_kernel,
        grid=(b, seq_len // tm),
        in_specs=[tok(D_MODEL), pl.BlockSpec((1, 6, D_MODEL), mod_map), _const_spec((1, D_MODEL)),
                  tok(Q_WIDTH), tok(POOL_WIDTH), tok(Q_WIDTH),
                  _const_spec((D_MODEL, 3 * D_MODEL)), _const_spec((Q_WIDTH, D_MODEL)),
                  _const_spec((POOL_WIDTH, D_MODEL)), _const_spec((Q_WIDTH, D_MODEL)),
                  _const_spec((D_MODEL, D_MODEL))],
        out_specs=tok(D_MODEL),
        out_shape=jax.ShapeDtypeStruct(x.shape, F32),
        compiler_params=_params(),
        name="merge",
    )(x, mod, gain, oa, ob, oc, w_gates, w_a, w_b, w_c, w_out)


FF_CHUNK = 1024


def _mlp_kernel(x_ref, mod_ref, g_ref, w1_ref, w2_ref, out_ref):
    mod = mod_ref[0]
    x = x_ref[0]
    h = _modulate(x, g_ref[...] * (1.0 + mod[4:5]), mod[3:4]).astype(BF16)
    acc = None
    for k in range(D_FF // FF_CHUNK):
        a = jnp.maximum(_dot(h, w1_ref[:, k * FF_CHUNK:(k + 1) * FF_CHUNK]), 0.0)
        part = _dot((a * a).astype(BF16), w2_ref[k * FF_CHUNK:(k + 1) * FF_CHUNK, :])
        acc = part if acc is None else acc + part
    out_ref[0] = x + mod[5:6] * acc


def _mlp(x, mod, gain, w1, w2, *, tm):
    b, seq_len, _ = x.shape
    mod_map = (lambda bi, i: (bi, 0, 0)) if mod.shape[0] == b else (lambda bi, i: (0, 0, 0))
    tok = pl.BlockSpec((1, tm, D_MODEL), lambda bi, i: (bi, i, 0))
    return pl.pallas_call(
        _mlp_kernel,
        grid=(b, seq_len // tm),
        in_specs=[tok, pl.BlockSpec((1, 6, D_MODEL), mod_map), _const_spec((1, D_MODEL)),
                  _const_spec((D_MODEL, D_FF)), _const_spec((D_FF, D_MODEL))],
        out_specs=tok,
        out_shape=jax.ShapeDtypeStruct(x.shape, F32),
        compiler_params=_params(),
        name="mlp",
    )(x, mod, gain, w1, w2)


def _rope_tables(n_tok):
    rows = n_tok // GRID_W
    r = jnp.repeat(jnp.arange(rows, dtype=F32), GRID_W)
    col = jnp.tile(jnp.arange(GRID_W, dtype=F32), rows)
    inv = 1.0 / (ROPE_THETA ** (jnp.arange(0, AXIS_DIM, 2, dtype=F32) / AXIS_DIM))
    ang = jnp.concatenate([r[:, None] * inv, col[:, None] * inv], axis=-1)
    pair = np.arange(LANES) % 32
    sign = np.where(np.arange(LANES) < LANES // 2, -1.0, 1.0).astype(np.float32)
    return jnp.cos(ang)[:, pair], jnp.sin(ang)[:, pair] * sign


def kernel(x, c, ctx, c_ctx, w_ada, b_ada, norm1, norm2, w_in, q_norm_a, k_norm_a, q_norm_c, k_norm_c, sink_c,
           w_pool, pool_scale, w_br_a, w_br_b, w_br_c, w_out, w_mlp1, w_mlp2):
    batch, seq_len, _ = x.shape
    n_ctx = ctx.shape[1]
    pad = (-(batch + 1)) % 8
    v = jnp.concatenate([c, c_ctx[None], jnp.zeros((pad, D_MODEL), F32)], axis=0)
    mods = _adaln(v, w_ada, b_ada)

    cos_lat, sin_lat = _rope_tables(seq_len)
    cos_ctx = jnp.ones((n_ctx, LANES), F32)
    sin_ctx = jnp.zeros((n_ctx, LANES), F32)
    q_scale = HEAD_DIM ** -0.5 * LOG2E

    xc = ctx
    for l in range(DEPTH):
        mod_lat = mods[l, :batch].reshape(batch, 6, D_MODEL)
        mod_ctx = mods[l, batch:batch + 1].reshape(1, 6, D_MODEL)
        w_proj = w_in[l][:, _PROJ_COLS].astype(BF16)
        w_gates = w_in[l][:, _GATES:].astype(BF16)
        head_gains = jnp.concatenate(
            [(q_norm_a[l] * q_scale)[_GP]] * N_GROUP + [(q_norm_c[l] * q_scale)[_GP]] * N_GROUP
            + [k_norm_a[l][_GP], k_norm_c[l][_GP]])[None]
        w_pool_bd = jax.scipy.linalg.block_diag(*[w_pool[l, g] for g in range(w_pool.shape[1])]).astype(BF16)
        ps = pool_scale[l][None]
        g1 = norm1[l][None]
        g2 = norm2[l][None]
        wa = w_br_a[l][_OP].astype(BF16)
        wb = w_br_b[l].astype(BF16)
        wc = w_br_c[l][_OP].astype(BF16)
        wo = w_out[l].astype(BF16)
        w1 = w_mlp1[l].astype(BF16)
        w2 = w_mlp2[l].astype(BF16)

        cqa, cqc, cka, ckc, cva, cvc, cob = _project(
            xc, mod_ctx, g1, w_proj, head_gains, cos_ctx, sin_ctx, w_pool_bd, ps, tm=n_ctx)
        qa, qc, ka, kc, va, vc, ob = _project(
            x, mod_lat, g1, w_proj, head_gains, cos_lat, sin_lat, w_pool_bd, ps, tm=512)

        oa = _attention(qa, cka, cva, ka, va)
        oc = _attention(qc, ckc, cvc, kc, vc, sink_c[l], window=True)
        x = _merge(x, mod_lat, g1, oa, ob, oc, w_gates, wa, wb, wc, wo, tm=512)
        x = _mlp(x, mod_lat, g2, w1, w2, tm=512)

        if l < DEPTH - 1:
            coa = _attention(cqa, cka, cva)
            coc = _attention(cqc, ckc, cvc, sink=sink_c[l])
            xc = _merge(xc, mod_ctx, g1, coa, cob, coc, w_gates, wa, wb, wc, wo, tm=n_ctx)
            xc = _mlp(xc, mod_ctx, g2, w1, w2, tm=n_ctx)
    return x
```

```python
import functools

import numpy as np
import jax
import jax.numpy as jnp
from jax import lax
from jax.experimental import pallas as pl
from jax.experimental.pallas import tpu as pltpu

D_MODEL = 1024
DEPTH = 2
GRID_W = 64
HEAD_DIM = 64
AXIS_DIM = HEAD_DIM // 2
ROPE_THETA = 10000.0
N_HEADS = 6
N_KV = 2
N_GROUP = N_HEADS // N_KV
Q_WIDTH = N_HEADS * HEAD_DIM
KV_WIDTH = N_KV * HEAD_DIM
POOL_WIDTH = 256
POOL_CH = 64
POOL_HALO = 8
WINDOW = 128
D_FF = 4 * D_MODEL
EPS = 1e-6
NEG = -1e30
LOG2E = 1.4426950408889634

LANES = 128
HALO_ROWS = 16
VMEM_LIMIT_BYTES = 56 * 1024 * 1024

F32 = jnp.float32
BF16 = jnp.bfloat16

_QA, _KA, _VA, _QC, _KC, _VC, _U, _GATES = 0, 384, 512, 640, 1024, 1152, 1280, 1536
PROJ_WIDTH = 2 * Q_WIDTH + 4 * KV_WIDTH + POOL_WIDTH


def _lane_parts(l):
    quarter, i = divmod(l, 32)
    return quarter % 2, quarter // 2, i


def _q_perm():
    idx = np.zeros(Q_WIDTH, np.int32)
    for c in range(N_GROUP):
        for l in range(LANES):
            slot, parity, i = _lane_parts(l)
            idx[c * LANES + l] = (c + N_GROUP * slot) * HEAD_DIM + 2 * i + parity
    return idx


def _k_perm():
    idx = np.zeros(KV_WIDTH, np.int32)
    for l in range(LANES):
        slot, parity, i = _lane_parts(l)
        idx[l] = slot * HEAD_DIM + 2 * i + parity
    return idx


def _gain_perm():
    return np.array([2 * _lane_parts(l)[2] + _lane_parts(l)[1] for l in range(LANES)], np.int32)


def _out_row_perm():
    idx = np.zeros(Q_WIDTH, np.int32)
    for c in range(N_GROUP):
        for half in range(2):
            for d in range(HEAD_DIM):
                idx[c * LANES + half * HEAD_DIM + d] = (c + N_GROUP * half) * HEAD_DIM + d
    return idx


_QP, _KP, _GP, _OP = _q_perm(), _k_perm(), _gain_perm(), _out_row_perm()
_PROJ_COLS = np.concatenate([
    _QA + _QP, _QC + _QP, _KA + _KP, _KC + _KP,
    np.arange(_VA, _VA + KV_WIDTH), np.arange(_VC, _VC + KV_WIDTH), np.arange(_U, _U + POOL_WIDTH)])
_HEAD_MEAN = np.array([[(a // LANES == b // LANES and _lane_parts(a % LANES)[0] == _lane_parts(b % LANES)[0])
                        / HEAD_DIM for b in range(2 * LANES)] for a in range(2 * LANES)], np.float32)
_ZQA, _ZQC, _ZKA, _ZKC, _ZVA, _ZVC, _ZU = 0, 384, 768, 896, 1024, 1152, 1280


def _const_spec(shape):
    nd = len(shape)
    return pl.BlockSpec(shape, lambda *_: (0,) * nd, pipeline_mode=pl.Buffered(1))


def _params():
    return pltpu.CompilerParams(dimension_semantics=("arbitrary", "arbitrary"),
                                vmem_limit_bytes=VMEM_LIMIT_BYTES)


def _modulate(x, gain_scale, shift):
    ms = jnp.mean(x * x, axis=-1, keepdims=True)
    return x * lax.rsqrt(ms + EPS) * gain_scale + shift


def _dot(a, b):
    return jnp.dot(a, b, preferred_element_type=F32)


def _adaln_kernel(v_ref, w_ref, b_ref, o_ref):
    v = v_ref[...]
    s = (v * jax.nn.sigmoid(v)).astype(BF16)
    o_ref[0] = _dot(s, w_ref[0].astype(BF16)) + b_ref[0]


def _adaln(v, w_ada, b_ada):
    rows = v.shape[0]
    tn = 1536
    return pl.pallas_call(
        _adaln_kernel,
        grid=(DEPTH, 6 * D_MODEL // tn),
        in_specs=[pl.BlockSpec((rows, D_MODEL), lambda l, j: (0, 0)),
                  pl.BlockSpec((1, D_MODEL, tn), lambda l, j: (l, 0, j)),
                  pl.BlockSpec((1, 1, tn), lambda l, j: (l, 0, j))],
        out_specs=pl.BlockSpec((1, rows, tn), lambda l, j: (l, 0, j)),
        out_shape=jax.ShapeDtypeStruct((DEPTH, rows, 6 * D_MODEL), F32),
        compiler_params=_params(),
        name="adaln",
    )(v, w_ada, b_ada.reshape(DEPTH, 1, 6 * D_MODEL))


def _proj_kernel(x_ref, xp_ref, xn_ref, mod_ref, g_ref, w_ref, hg_ref, hm_ref, cos_ref, sin_ref, wp_ref, ps_ref,
                 qa_ref, qc_ref, ka_ref, kc_ref, va_ref, vc_ref, ob_ref, a0_ref, a1_ref, a2_ref, a3_ref,
                 *, tm, nsub, seq_len):
    i = pl.program_id(1)
    mod = mod_ref[0]
    shift = mod[0:1]
    gain_scale = g_ref[...] * (1.0 + mod[1:2])
    w = w_ref[...]
    hm = hm_ref[...]
    outs = ([(qa_ref, c) for c in range(N_GROUP)] + [(qc_ref, c) for c in range(N_GROUP)]
            + [(ka_ref, 0), (kc_ref, 0)])
    n0 = tm + 2 * POOL_HALO
    sub = tm // nsub
    for t in range(nsub):
        rows = slice(t * sub, (t + 1) * sub)
        h = _modulate(x_ref[0, rows], gain_scale, shift).astype(BF16)
        z = _dot(h, w)
        cos = cos_ref[rows]
        sin = sin_ref[rows]
        for p in range(4):
            zp = z[:, 2 * LANES * p:2 * LANES * (p + 1)]
            sq = zp * zp
            hi = sq.astype(BF16)
            lo = (sq - hi.astype(F32)).astype(BF16)
            ms = _dot(hi, hm) + _dot(lo, hm)
            y = zp * lax.rsqrt(ms + EPS) * hg_ref[:, 2 * LANES * p:2 * LANES * (p + 1)]
            for half in range(2):
                yc = y[:, half * LANES:(half + 1) * LANES]
                ref, c = outs[2 * p + half]
                ref[0, rows, c * LANES:(c + 1) * LANES] = (
                    yc * cos + pltpu.roll(yc, LANES // 2, 1) * sin).astype(BF16)
        va_ref[0, rows] = z[:, _ZVA:_ZVA + LANES].astype(BF16)
        vc_ref[0, rows] = z[:, _ZVC:_ZVC + LANES].astype(BF16)
        a0_ref[POOL_HALO + t * sub:POOL_HALO + (t + 1) * sub] = z[:, _ZU:_ZU + POOL_WIDTH]

    wu = w[:, _ZU:_ZU + POOL_WIDTH]

    def halo(ref, valid):
        hh = _modulate(ref[0], gain_scale, shift).astype(BF16)
        return jnp.where(valid, _dot(hh, wu), 0.0)

    def put(ref, val):
        ref[0:n0] = val
        ref[n0:n0 + POOL_HALO] = jnp.zeros((POOL_HALO, val.shape[1]), F32)

    a0_ref[0:POOL_HALO] = halo(xp_ref, i > 0)[HALO_ROWS - POOL_HALO:]
    a0_ref[POOL_HALO + tm:n0] = halo(xn_ref, i < pl.num_programs(1) - 1)[:POOL_HALO]
    u = a0_ref[POOL_HALO:POOL_HALO + tm]
    a0_ref[n0:n0 + POOL_HALO] = jnp.zeros((POOL_HALO, POOL_WIDTH), F32)
    w2 = a0_ref[0:n0] + a0_ref[1:n0 + 1]
    put(a1_ref, w2)
    w4 = w2 + a1_ref[2:n0 + 2]
    put(a2_ref, w4)
    w8 = w4[:, LANES:] + a2_ref[4:n0 + 4, LANES:]
    put(a3_ref, w8)
    w16 = w8 + a3_ref[8:n0 + 8]
    lane = lax.broadcasted_iota(jnp.int32, (1, LANES), 1)
    first = lane < POOL_CH
    t = i * tm + lax.broadcasted_iota(jnp.int32, (tm, 1), 0)
    pooled = []
    for col, (narrow, wide) in enumerate(((a1_ref[7:7 + tm, :LANES], a2_ref[6:6 + tm, :LANES]),
                                          (a3_ref[4:4 + tm], w16[0:tm]))):
        half = jnp.where(first, 1, 2) << (2 * col)
        cnt = (jnp.minimum(t + half, seq_len) - jnp.maximum(t - half, 0)).astype(F32)
        pooled.append(jnp.where(first, narrow, wide) / cnt - u[:, col * LANES:(col + 1) * LANES])
    pooled = jnp.concatenate(pooled, axis=1)
    ob_ref[0] = (_dot(pooled.astype(BF16), wp_ref[...]) * ps_ref[...]).astype(BF16)


def _project(x, mod, gain, w_proj, head_gains, cos_tab, sin_tab, w_pool_bd, pool_scale, *, tm, nsub=1):
    b, seq_len, _ = x.shape
    nb = seq_len // tm
    hb = tm // HALO_ROWS
    mod_map = (lambda bi, i: (bi, 0, 0)) if mod.shape[0] == b else (lambda bi, i: (0, 0, 0))
    tok = lambda width: pl.BlockSpec((1, tm, width), lambda bi, i: (bi, i, 0))
    out = lambda width: jax.ShapeDtypeStruct((b, seq_len, width), BF16)
    return pl.pallas_call(
        functools.partial(_proj_kernel, tm=tm, nsub=nsub, seq_len=seq_len),
        grid=(b, nb),
        in_specs=[tok(D_MODEL),
                  pl.BlockSpec((1, HALO_ROWS, D_MODEL), lambda bi, i: (bi, jnp.maximum(i * hb - 1, 0), 0)),
                  pl.BlockSpec((1, HALO_ROWS, D_MODEL),
                               lambda bi, i: (bi, jnp.minimum((i + 1) * hb, nb * hb - 1), 0)),
                  pl.BlockSpec((1, 6, D_MODEL), mod_map),
                  _const_spec((1, D_MODEL)),
                  _const_spec((D_MODEL, PROJ_WIDTH)),
                  _const_spec((1, 8 * LANES)),
                  _const_spec((2 * LANES, 2 * LANES)),
                  pl.BlockSpec((tm, LANES), lambda bi, i: (i, 0)),
                  pl.BlockSpec((tm, LANES), lambda bi, i: (i, 0)),
                  _const_spec((POOL_WIDTH, POOL_WIDTH)),
                  _const_spec((1, POOL_WIDTH))],
        out_specs=[tok(Q_WIDTH), tok(Q_WIDTH), tok(KV_WIDTH), tok(KV_WIDTH), tok(KV_WIDTH), tok(KV_WIDTH),
                   tok(POOL_WIDTH)],
        out_shape=[out(Q_WIDTH), out(Q_WIDTH), out(KV_WIDTH), out(KV_WIDTH), out(KV_WIDTH), out(KV_WIDTH),
                   out(POOL_WIDTH)],
        scratch_shapes=[pltpu.VMEM((tm + 3 * POOL_HALO, POOL_WIDTH), F32)] * 3
                       + [pltpu.VMEM((tm + 3 * POOL_HALO, LANES), F32)],
        compiler_params=_params(),
        name="project",
    )(x, x, x, mod, gain, w_proj, head_gains, jnp.asarray(_HEAD_MEAN, BF16), cos_tab, sin_tab, w_pool_bd,
      pool_scale)


_NT = (((1,), (1,)), ((), ()))
SKEW = 1
MAX_EXP2_SPAN = 80.0
NORM_MARGIN = 1.02


def _score_bound(q_gain, k_gain):
    return HEAD_DIM * jnp.max(jnp.abs(q_gain)) * jnp.max(jnp.abs(k_gain)) * NORM_MARGIN


def _attn_kernel(aux_ref, *refs, tq, nsub, has_lat, window, has_sink, bounded, seq_len):
    q_ref, kc_ref, vc_ref = refs[:3]
    if has_lat:
        kl_ref, vl_ref = refs[3:5]
    o_ref = refs[-1]
    i = pl.program_id(1)

    lane = lax.broadcasted_iota(jnp.int32, (1, LANES), 1)
    slot0 = ((lane >> 5) & 1) == 0
    zero = jnp.zeros((), BF16)
    kc = kc_ref[0]
    vc = vc_ref[0]
    lat = []
    for t in range(nsub if has_lat else 0):
        if window:
            nk = tq + 2 * WINDOW
            q0 = (i * nsub + t) * tq
            start = pl.multiple_of(jnp.clip(q0 - WINDOW, 0, seq_len - nk), WINDOW)
            r = lax.broadcasted_iota(jnp.int32, (tq, nk), 0)
            c = lax.broadcasted_iota(jnp.int32, (tq, nk), 1)
            lat.append((kl_ref[0, pl.ds(start, nk), :], vl_ref[0, pl.ds(start, nk), :],
                        jnp.where(jnp.abs(q0 - start + r - c) <= WINDOW, 0.0, NEG)))
        else:
            lat.append((kl_ref[0], vl_ref[0], None))

    def scores(t, j):
        kv, g = divmod(j, N_GROUP)
        qj = jnp.where(slot0 if kv == 0 else ~slot0, q_ref[0, t * tq:(t + 1) * tq, g * LANES:(g + 1) * LANES], zero)
        s_ctx = lax.dot_general(qj, kc, _NT, preferred_element_type=F32)
        m = aux_ref[6] if bounded else jnp.max(s_ctx, axis=-1, keepdims=True)
        s_lat = None
        if has_lat:
            kl, _, mask_bias = lat[t]
            s_lat = lax.dot_general(qj, kl, _NT, preferred_element_type=F32)
            if window:
                s_lat = s_lat + mask_bias
            if not bounded:
                m = jnp.maximum(m, jnp.max(s_lat, axis=-1, keepdims=True))
        return s_ctx, s_lat, m

    def weighted_values(t, j, s_ctx, s_lat, m):
        if has_sink and not bounded:
            m = jnp.maximum(m, aux_ref[j])
        p_ctx = jnp.exp2(s_ctx - m)
        denom = jnp.sum(p_ctx, axis=-1, keepdims=True)
        o = _dot(p_ctx.astype(BF16), vc)
        if has_lat:
            p_lat = jnp.exp2(s_lat - m)
            denom = denom + jnp.sum(p_lat, axis=-1, keepdims=True)
            o = o + _dot(p_lat.astype(BF16), lat[t][1])
        if has_sink:
            denom = denom + jnp.exp2(jnp.full((1, 1), aux_ref[j], F32) - m)
        return o / denom

    units = [(t, j) for t in range(nsub) for j in range(N_HEADS)]
    pending = [scores(*u) for u in units[:SKEW]]
    outs = {}
    for n, u in enumerate(units):
        if n + SKEW < len(units):
            pending.append(scores(*units[n + SKEW]))
        outs[u] = weighted_values(*u, *pending.pop(0))
    for t in range(nsub):
        for g in range(N_GROUP):
            blk = jnp.where(lane < HEAD_DIM, outs[t, g], outs[t, N_GROUP + g])
            o_ref[0, t * tq:(t + 1) * tq, g * LANES:(g + 1) * LANES] = blk.astype(BF16)


def _attention(q, k_ctx, v_ctx, k_lat=None, v_lat=None, sink=None, *, score_bound, window=False, tq=256, nsub=1):
    b, nq, _ = q.shape
    n_ctx = k_ctx.shape[1]
    has_lat = k_lat is not None
    has_sink = sink is not None
    rows = tq * nsub
    sink2 = sink.astype(F32) * LOG2E if has_sink else jnp.zeros((N_HEADS,), F32)
    shift = jnp.maximum(score_bound, jnp.max(sink2)) if has_sink else score_bound
    aux = jnp.concatenate([sink2, shift[None], jnp.zeros((1,), F32)])
    args = [aux, q, k_ctx, v_ctx]
    specs = [pl.BlockSpec(memory_space=pltpu.SMEM),
             pl.BlockSpec((1, rows, Q_WIDTH), lambda bi, i: (bi, i, 0)),
             pl.BlockSpec((1, n_ctx, KV_WIDTH), lambda bi, i: (bi, 0, 0)),
             pl.BlockSpec((1, n_ctx, KV_WIDTH), lambda bi, i: (bi, 0, 0))]
    if has_lat:
        n_lat = k_lat.shape[1]
        args += [k_lat, v_lat]
        specs += [pl.BlockSpec((1, n_lat, KV_WIDTH), lambda bi, i: (bi, 0, 0)),
                  pl.BlockSpec((1, n_lat, KV_WIDTH), lambda bi, i: (bi, 0, 0))]
    name = "window_attn" if window else ("global_attn" if has_lat else "ctx_attn")

    def call(bounded):
        return pl.pallas_call(
            functools.partial(_attn_kernel, tq=tq, nsub=nsub, has_lat=has_lat, window=window, has_sink=has_sink,
                              bounded=bounded, seq_len=nq),
            grid=(b, nq // rows),
            in_specs=specs,
            out_specs=pl.BlockSpec((1, rows, Q_WIDTH), lambda bi, i: (bi, i, 0)),
            out_shape=jax.ShapeDtypeStruct((b, nq, Q_WIDTH), BF16),
            compiler_params=_params(),
            name=name if bounded else name + "_rowmax",
        )(*args)

    return lax.cond(score_bound + shift <= MAX_EXP2_SPAN, lambda: call(True), lambda: call(False))


def _merge_kernel(x_ref, mod_ref, g_ref, oa_ref, ob_ref, oc_ref, wg_ref, wa_ref, wb_ref, wc_ref, wo_ref, out_ref):
    mod = mod_ref[0]
    x = x_ref[0]
    h = _modulate(x, g_ref[...] * (1.0 + mod[1:2]), mod[0:1]).astype(BF16)
    y = None
    for k, (o_ref, w_ref) in enumerate(((oa_ref, wa_ref), (ob_ref, wb_ref), (oc_ref, wc_ref))):
        gate = jax.nn.sigmoid(_dot(h, wg_ref[:, k * D_MODEL:(k + 1) * D_MODEL]))
        term = gate * _dot(o_ref[0], w_ref[...])
        y = term if y is None else y + term
    out_ref[0] = x + mod[2:3] * _dot(y.astype(BF16), wo_ref[...])


def _merge(x, mod, gain, oa, ob, oc, w_gates, w_a, w_b, w_c, w_out, *, tm):
    b, seq_len, _ = x.shape
    mod_map = (lambda bi, i: (bi, 0, 0)) if mod.shape[0] == b else (lambda bi, i: (0, 0, 0))
    tok = lambda width: pl.BlockSpec((1, tm, width), lambda bi, i: (bi, i, 0))
    return pl.pallas_call(
        _merge_kernel,
        grid=(b, seq_len // tm),
        in_specs=[tok(D_MODEL), pl.BlockSpec((1, 6, D_MODEL), mod_map), _const_spec((1, D_MODEL)),
                  tok(Q_WIDTH), tok(POOL_WIDTH), tok(Q_WIDTH),
                  _const_spec((D_MODEL, 3 * D_MODEL)), _const_spec((Q_WIDTH, D_MODEL)),
                  _const_spec((POOL_WIDTH, D_MODEL)), _const_spec((Q_WIDTH, D_MODEL)),
                  _const_spec((D_MODEL, D_MODEL))],
        out_specs=tok(D_MODEL),
        out_shape=jax.ShapeDtypeStruct(x.shape, F32),
        compiler_params=_params(),
        name="merge",
    )(x, mod, gain, oa, ob, oc, w_gates, w_a, w_b, w_c, w_out)


FF_CHUNK = 1024


def _mlp_kernel(x_ref, mod_ref, g_ref, w1_ref, w2_ref, out_ref):
    mod = mod_ref[0]
    x = x_ref[0]
    h = _modulate(x, g_ref[...] * (1.0 + mod[4:5]), mod[3:4]).astype(BF16)
    acc = None
    for k in range(D_FF // FF_CHUNK):
        a = jnp.maximum(_dot(h, w1_ref[:, k * FF_CHUNK:(k + 1) * FF_CHUNK]), 0.0)
        part = _dot((a * a).astype(BF16), w2_ref[k * FF_CHUNK:(k + 1) * FF_CHUNK, :])
        acc = part if acc is None else acc + part
    out_ref[0] = x + mod[5:6] * acc


def _mlp(x, mod, gain, w1, w2, *, tm):
    b, seq_len, _ = x.shape
    mod_map = (lambda bi, i: (bi, 0, 0)) if mod.shape[0] == b else (lambda bi, i: (0, 0, 0))
    tok = pl.BlockSpec((1, tm, D_MODEL), lambda bi, i: (bi, i, 0))
    return pl.pallas_call(
        _mlp_kernel,
        grid=(b, seq_len // tm),
        in_specs=[tok, pl.BlockSpec((1, 6, D_MODEL), mod_map), _const_spec((1, D_MODEL)),
                  _const_spec((D_MODEL, D_FF)), _const_spec((D_FF, D_MODEL))],
        out_specs=tok,
        out_shape=jax.ShapeDtypeStruct(x.shape, F32),
        compiler_params=_params(),
        name="mlp",
    )(x, mod, gain, w1, w2)


def _rope_tables(n_tok):
    rows = n_tok // GRID_W
    r = jnp.repeat(jnp.arange(rows, dtype=F32), GRID_W)
    col = jnp.tile(jnp.arange(GRID_W, dtype=F32), rows)
    inv = 1.0 / (ROPE_THETA ** (jnp.arange(0, AXIS_DIM, 2, dtype=F32) / AXIS_DIM))
    ang = jnp.concatenate([r[:, None] * inv, col[:, None] * inv], axis=-1)
    pair = np.arange(LANES) % 32
    sign = np.where(np.arange(LANES) < LANES // 2, -1.0, 1.0).astype(np.float32)
    return jnp.cos(ang)[:, pair], jnp.sin(ang)[:, pair] * sign


def kernel(x, c, ctx, c_ctx, w_ada, b_ada, norm1, norm2, w_in, q_norm_a, k_norm_a, q_norm_c, k_norm_c, sink_c,
           w_pool, pool_scale, w_br_a, w_br_b, w_br_c, w_out, w_mlp1, w_mlp2):
    batch, seq_len, _ = x.shape
    n_ctx = ctx.shape[1]
    pad = (-(batch + 1)) % 8
    v = jnp.concatenate([c, c_ctx[None], jnp.zeros((pad, D_MODEL), F32)], axis=0)
    mods = _adaln(v, w_ada, b_ada)

    cos_lat, sin_lat = _rope_tables(seq_len)
    cos_ctx = jnp.ones((n_ctx, LANES), F32)
    sin_ctx = jnp.zeros((n_ctx, LANES), F32)
    q_scale = HEAD_DIM ** -0.5 * LOG2E

    xc = ctx
    for l in range(DEPTH):
        mod_lat = mods[l, :batch].reshape(batch, 6, D_MODEL)
        mod_ctx = mods[l, batch:batch + 1].reshape(1, 6, D_MODEL)
        w_proj = w_in[l][:, _PROJ_COLS].astype(BF16)
        w_gates = w_in[l][:, _GATES:].astype(BF16)
        head_gains = jnp.concatenate(
            [(q_norm_a[l] * q_scale)[_GP]] * N_GROUP + [(q_norm_c[l] * q_scale)[_GP]] * N_GROUP
            + [k_norm_a[l][_GP], k_norm_c[l][_GP]])[None]
        w_pool_bd = jax.scipy.linalg.block_diag(*[w_pool[l, g] for g in range(w_pool.shape[1])]).astype(BF16)
        ps = pool_scale[l][None]
        g1 = norm1[l][None]
        g2 = norm2[l][None]
        wa = w_br_a[l][_OP].astype(BF16)
        wb = w_br_b[l].astype(BF16)
        wc = w_br_c[l][_OP].astype(BF16)
        wo = w_out[l].astype(BF16)
        w1 = w_mlp1[l].astype(BF16)
        w2 = w_mlp2[l].astype(BF16)

        cqa, cqc, cka, ckc, cva, cvc, cob = _project(
            xc, mod_ctx, g1, w_proj, head_gains, cos_ctx, sin_ctx, w_pool_bd, ps, tm=n_ctx)
        qa, qc, ka, kc, va, vc, ob = _project(
            x, mod_lat, g1, w_proj, head_gains, cos_lat, sin_lat, w_pool_bd, ps, tm=512, nsub=2)

        bound_a = _score_bound(q_norm_a[l] * q_scale, k_norm_a[l])
        bound_c = _score_bound(q_norm_c[l] * q_scale, k_norm_c[l])
        oa = _attention(qa, cka, cva, ka, va, score_bound=bound_a, nsub=2)
        oc = _attention(qc, ckc, cvc, kc, vc, sink_c[l], score_bound=bound_c, window=True, nsub=2)
        x = _merge(x, mod_lat, g1, oa, ob, oc, w_gates, wa, wb, wc, wo, tm=512)
        x = _mlp(x, mod_lat, g2, w1, w2, tm=512)

        if l < DEPTH - 1:
            coa = _attention(cqa, cka, cva, score_bound=bound_a)
            coc = _attention(cqc, ckc, cvc, sink=sink_c[l], score_bound=bound_c)
            xc = _merge(xc, mod_ctx, g1, coa, cob, coc, w_gates, wa, wb, wc, wo, tm=n_ctx)
            xc = _mlp(xc, mod_ctx, g2, w1, w2, tm=n_ctx)
    return x
```

```python
import functools

import numpy as np
import jax
import jax.numpy as jnp
from jax import lax
from jax.experimental import pallas as pl
from jax.experimental.pallas import tpu as pltpu

D_MODEL = 1024
DEPTH = 2
GRID_W = 64
HEAD_DIM = 64
AXIS_DIM = HEAD_DIM // 2
ROPE_THETA = 10000.0
N_HEADS = 6
N_KV = 2
N_GROUP = N_HEADS // N_KV
Q_WIDTH = N_HEADS * HEAD_DIM
KV_WIDTH = N_KV * HEAD_DIM
POOL_WIDTH = 256
POOL_CH = 64
POOL_HALO = 8
WINDOW = 128
D_FF = 4 * D_MODEL
EPS = 1e-6
NEG = -1e30
LOG2E = 1.4426950408889634

LANES = 128
HALO_ROWS = 16
VMEM_LIMIT_BYTES = 56 * 1024 * 1024

F32 = jnp.float32
BF16 = jnp.bfloat16

_QA, _KA, _VA, _QC, _KC, _VC, _U, _GATES = 0, 384, 512, 640, 1024, 1152, 1280, 1536
PROJ_WIDTH = 2 * Q_WIDTH + 4 * KV_WIDTH + POOL_WIDTH


def _lane_parts(l):
    quarter, i = divmod(l, 32)
    return quarter % 2, quarter // 2, i


def _q_perm():
    idx = np.zeros(Q_WIDTH, np.int32)
    for c in range(N_GROUP):
        for l in range(LANES):
            slot, parity, i = _lane_parts(l)
            idx[c * LANES + l] = (c + N_GROUP * slot) * HEAD_DIM + 2 * i + parity
    return idx


def _k_perm():
    idx = np.zeros(KV_WIDTH, np.int32)
    for l in range(LANES):
        slot, parity, i = _lane_parts(l)
        idx[l] = slot * HEAD_DIM + 2 * i + parity
    return idx


def _gain_perm():
    return np.array([2 * _lane_parts(l)[2] + _lane_parts(l)[1] for l in range(LANES)], np.int32)


def _out_row_perm():
    idx = np.zeros(Q_WIDTH, np.int32)
    for c in range(N_GROUP):
        for half in range(2):
            for d in range(HEAD_DIM):
                idx[c * LANES + half * HEAD_DIM + d] = (c + N_GROUP * half) * HEAD_DIM + d
    return idx


_QP, _KP, _GP, _OP = _q_perm(), _k_perm(), _gain_perm(), _out_row_perm()
_PROJ_COLS = np.concatenate([
    _QA + _QP, _QC + _QP, _KA + _KP, _KC + _KP,
    np.arange(_VA, _VA + KV_WIDTH), np.arange(_VC, _VC + KV_WIDTH), np.arange(_U, _U + POOL_WIDTH)])
_HEAD_MEAN = np.array([[(a // LANES == b // LANES and _lane_parts(a % LANES)[0] == _lane_parts(b % LANES)[0])
                        / HEAD_DIM for b in range(2 * LANES)] for a in range(2 * LANES)], np.float32)
_ZQA, _ZQC, _ZKA, _ZKC, _ZVA, _ZVC, _ZU = 0, 384, 768, 896, 1024, 1152, 1280


def _const_spec(shape, layer=None):
    nd = len(shape)
    if layer is None:
        return pl.BlockSpec(shape, lambda *_: (0,) * nd, pipeline_mode=pl.Buffered(1))
    return pl.BlockSpec((None,) + shape, lambda *_: (layer,) + (0,) * nd, pipeline_mode=pl.Buffered(1))


def _params():
    return pltpu.CompilerParams(dimension_semantics=("arbitrary", "arbitrary"),
                                vmem_limit_bytes=VMEM_LIMIT_BYTES)


def _modulate(x, gain_scale, shift):
    ms = jnp.mean(x * x, axis=-1, keepdims=True)
    return x * lax.rsqrt(ms + EPS) * gain_scale + shift


def _dot(a, b):
    return jnp.dot(a, b, preferred_element_type=F32)


def _adaln_kernel(v_ref, w_ref, b_ref, o_ref):
    v = v_ref[...]
    s = (v * jax.nn.sigmoid(v)).astype(BF16)
    o_ref[0] = _dot(s, w_ref[0].astype(BF16)) + b_ref[0]


def _adaln(v, w_ada, b_ada):
    rows = v.shape[0]
    tn = 1536
    return pl.pallas_call(
        _adaln_kernel,
        grid=(DEPTH, 6 * D_MODEL // tn),
        in_specs=[pl.BlockSpec((rows, D_MODEL), lambda l, j: (0, 0)),
                  pl.BlockSpec((1, D_MODEL, tn), lambda l, j: (l, 0, j)),
                  pl.BlockSpec((1, 1, tn), lambda l, j: (l, 0, j))],
        out_specs=pl.BlockSpec((1, rows, tn), lambda l, j: (l, 0, j)),
        out_shape=jax.ShapeDtypeStruct((DEPTH, rows, 6 * D_MODEL), F32),
        compiler_params=_params(),
        name="adaln",
    )(v, w_ada, b_ada.reshape(DEPTH, 1, 6 * D_MODEL))


def _proj_kernel(x_ref, xp_ref, xn_ref, mod_ref, g_ref, w_ref, hg_ref, hm_ref, cos_ref, sin_ref, wp_ref, ps_ref,
                 qa_ref, qc_ref, ka_ref, kc_ref, va_ref, vc_ref, ob_ref, a0_ref, a1_ref, a2_ref, a3_ref,
                 *, tm, nsub, seq_len):
    i = pl.program_id(1)
    mod = mod_ref[0]
    shift = mod[0:1]
    gain_scale = g_ref[...] * (1.0 + mod[1:2])
    w = w_ref[...]
    hm = hm_ref[...]
    outs = ([(qa_ref, c) for c in range(N_GROUP)] + [(qc_ref, c) for c in range(N_GROUP)]
            + [(ka_ref, 0), (kc_ref, 0)])
    n0 = tm + 2 * POOL_HALO
    sub = tm // nsub
    def project_rows(t):
        h = _modulate(x_ref[0, t * sub:(t + 1) * sub], gain_scale, shift).astype(BF16)
        return _dot(h, w)

    zs = [project_rows(t) for t in range(nsub)]
    for t, z in enumerate(zs):
        rows = slice(t * sub, (t + 1) * sub)
        cos = cos_ref[rows]
        sin = sin_ref[rows]
        for p in range(4):
            zp = z[:, 2 * LANES * p:2 * LANES * (p + 1)]
            sq = zp * zp
            hi = sq.astype(BF16)
            lo = (sq - hi.astype(F32)).astype(BF16)
            ms = _dot(hi, hm) + _dot(lo, hm)
            y = zp * lax.rsqrt(ms + EPS) * hg_ref[:, 2 * LANES * p:2 * LANES * (p + 1)]
            for half in range(2):
                yc = y[:, half * LANES:(half + 1) * LANES]
                ref, c = outs[2 * p + half]
                ref[0, rows, c * LANES:(c + 1) * LANES] = (
                    yc * cos + pltpu.roll(yc, LANES // 2, 1) * sin).astype(BF16)
        va_ref[0, rows] = z[:, _ZVA:_ZVA + LANES].astype(BF16)
        vc_ref[0, rows] = z[:, _ZVC:_ZVC + LANES].astype(BF16)
        a0_ref[POOL_HALO + t * sub:POOL_HALO + (t + 1) * sub] = z[:, _ZU:_ZU + POOL_WIDTH]

    wu = w[:, _ZU:_ZU + POOL_WIDTH]

    def halo(ref, valid):
        hh = _modulate(ref[0], gain_scale, shift).astype(BF16)
        return jnp.where(valid, _dot(hh, wu), 0.0)

    def put(ref, val):
        ref[0:n0] = val
        ref[n0:n0 + POOL_HALO] = jnp.zeros((POOL_HALO, val.shape[1]), F32)

    a0_ref[0:POOL_HALO] = halo(xp_ref, i > 0)[HALO_ROWS - POOL_HALO:]
    a0_ref[POOL_HALO + tm:n0] = halo(xn_ref, i < pl.num_programs(1) - 1)[:POOL_HALO]
    u = a0_ref[POOL_HALO:POOL_HALO + tm]
    a0_ref[n0:n0 + POOL_HALO] = jnp.zeros((POOL_HALO, POOL_WIDTH), F32)
    w2 = a0_ref[0:n0] + a0_ref[1:n0 + 1]
    put(a1_ref, w2)
    w4 = w2 + a1_ref[2:n0 + 2]
    put(a2_ref, w4)
    w8 = w4[:, LANES:] + a2_ref[4:n0 + 4, LANES:]
    put(a3_ref, w8)
    w16 = w8 + a3_ref[8:n0 + 8]
    lane = lax.broadcasted_iota(jnp.int32, (1, LANES), 1)
    first = lane < POOL_CH
    t = i * tm + lax.broadcasted_iota(jnp.int32, (tm, 1), 0)
    pooled = []
    for col, (narrow, wide) in enumerate(((a1_ref[7:7 + tm, :LANES], a2_ref[6:6 + tm, :LANES]),
                                          (a3_ref[4:4 + tm], w16[0:tm]))):
        half = jnp.where(first, 1, 2) << (2 * col)
        cnt = (jnp.minimum(t + half, seq_len) - jnp.maximum(t - half, 0)).astype(F32)
        pooled.append(jnp.where(first, narrow, wide) / cnt - u[:, col * LANES:(col + 1) * LANES])
    pooled = jnp.concatenate(pooled, axis=1)
    ob_ref[0] = (_dot(pooled.astype(BF16), wp_ref[...]) * ps_ref[...]).astype(BF16)


def _project(x, mod, gain, w_proj, head_gains, cos_tab, sin_tab, w_pool_bd, pool_scale, *, layer, tm, nsub=1):
    b, seq_len, _ = x.shape
    nb = seq_len // tm
    hb = tm // HALO_ROWS
    mod_map = (lambda bi, i: (bi, 0, 0)) if mod.shape[0] == b else (lambda bi, i: (0, 0, 0))
    tok = lambda width: pl.BlockSpec((1, tm, width), lambda bi, i: (bi, i, 0))
    out = lambda width: jax.ShapeDtypeStruct((b, seq_len, width), BF16)
    return pl.pallas_call(
        functools.partial(_proj_kernel, tm=tm, nsub=nsub, seq_len=seq_len),
        grid=(b, nb),
        in_specs=[tok(D_MODEL),
                  pl.BlockSpec((1, HALO_ROWS, D_MODEL), lambda bi, i: (bi, jnp.maximum(i * hb - 1, 0), 0)),
                  pl.BlockSpec((1, HALO_ROWS, D_MODEL),
                               lambda bi, i: (bi, jnp.minimum((i + 1) * hb, nb * hb - 1), 0)),
                  pl.BlockSpec((1, 6, D_MODEL), mod_map),
                  _const_spec((1, D_MODEL)),
                  _const_spec((D_MODEL, PROJ_WIDTH), layer),
                  _const_spec((1, 8 * LANES)),
                  _const_spec((2 * LANES, 2 * LANES)),
                  pl.BlockSpec((tm, LANES), lambda bi, i: (i, 0)),
                  pl.BlockSpec((tm, LANES), lambda bi, i: (i, 0)),
                  _const_spec((POOL_WIDTH, POOL_WIDTH)),
                  _const_spec((1, POOL_WIDTH))],
        out_specs=[tok(Q_WIDTH), tok(Q_WIDTH), tok(KV_WIDTH), tok(KV_WIDTH), tok(KV_WIDTH), tok(KV_WIDTH),
                   tok(POOL_WIDTH)],
        out_shape=[out(Q_WIDTH), out(Q_WIDTH), out(KV_WIDTH), out(KV_WIDTH), out(KV_WIDTH), out(KV_WIDTH),
                   out(POOL_WIDTH)],
        scratch_shapes=[pltpu.VMEM((tm + 3 * POOL_HALO, POOL_WIDTH), F32)] * 3
                       + [pltpu.VMEM((tm + 3 * POOL_HALO, LANES), F32)],
        compiler_params=_params(),
        name="project",
    )(x, x, x, mod, gain, w_proj, head_gains, jnp.asarray(_HEAD_MEAN, BF16), cos_tab, sin_tab, w_pool_bd,
      pool_scale)


_NT = (((1,), (1,)), ((), ()))
SKEW = 1
MAX_EXP2_SPAN = 80.0
NORM_MARGIN = 1.02


def _score_bound(q_gain, k_gain):
    return HEAD_DIM * jnp.max(jnp.abs(q_gain)) * jnp.max(jnp.abs(k_gain)) * NORM_MARGIN


def _attn_kernel(aux_ref, *refs, tq, nsub, has_lat, window, has_sink, bounded, seq_len):
    q_ref, kc_ref, vc_ref = refs[:3]
    if has_lat:
        kl_ref, vl_ref = refs[3:5]
    o_ref = refs[-1]
    i = pl.program_id(1)

    lane = lax.broadcasted_iota(jnp.int32, (1, LANES), 1)
    slot0 = ((lane >> 5) & 1) == 0
    zero = jnp.zeros((), BF16)
    kc = kc_ref[0]
    vc = vc_ref[0]
    lat = []
    for t in range(nsub if has_lat else 0):
        if window:
            nk = tq + 2 * WINDOW
            q0 = (i * nsub + t) * tq
            start = pl.multiple_of(jnp.clip(q0 - WINDOW, 0, seq_len - nk), WINDOW)
            r = lax.broadcasted_iota(jnp.int32, (tq, nk), 0)
            c = lax.broadcasted_iota(jnp.int32, (tq, nk), 1)
            lat.append((kl_ref[0, pl.ds(start, nk), :], vl_ref[0, pl.ds(start, nk), :],
                        jnp.where(jnp.abs(q0 - start + r - c) <= WINDOW, 0.0, NEG)))
        else:
            lat.append((kl_ref[0], vl_ref[0], None))

    def scores(t, j):
        kv, g = divmod(j, N_GROUP)
        qj = jnp.where(slot0 if kv == 0 else ~slot0, q_ref[0, t * tq:(t + 1) * tq, g * LANES:(g + 1) * LANES], zero)
        s_ctx = lax.dot_general(qj, kc, _NT, preferred_element_type=F32)
        m = aux_ref[6] if bounded else jnp.max(s_ctx, axis=-1, keepdims=True)
        s_lat = None
        if has_lat:
            kl, _, mask_bias = lat[t]
            s_lat = lax.dot_general(qj, kl, _NT, preferred_element_type=F32)
            if window:
                s_lat = s_lat + mask_bias
            if not bounded:
                m = jnp.maximum(m, jnp.max(s_lat, axis=-1, keepdims=True))
        return s_ctx, s_lat, m

    def weighted_values(t, j, s_ctx, s_lat, m):
        if has_sink and not bounded:
            m = jnp.maximum(m, aux_ref[j])
        p_ctx = jnp.exp2(s_ctx - m)
        denom = jnp.sum(p_ctx, axis=-1, keepdims=True)
        o = _dot(p_ctx.astype(BF16), vc)
        if has_lat:
            p_lat = jnp.exp2(s_lat - m)
            denom = denom + jnp.sum(p_lat, axis=-1, keepdims=True)
            o = o + _dot(p_lat.astype(BF16), lat[t][1])
        if has_sink:
            denom = denom + jnp.exp2(jnp.full((1, 1), aux_ref[j], F32) - m)
        return o / denom

    units = [(t, j) for t in range(nsub) for j in range(N_HEADS)]
    pending = [scores(*u) for u in units[:SKEW]]
    outs = {}
    for n, u in enumerate(units):
        if n + SKEW < len(units):
            pending.append(scores(*units[n + SKEW]))
        outs[u] = weighted_values(*u, *pending.pop(0))
    for t in range(nsub):
        for g in range(N_GROUP):
            blk = jnp.where(lane < HEAD_DIM, outs[t, g], outs[t, N_GROUP + g])
            o_ref[0, t * tq:(t + 1) * tq, g * LANES:(g + 1) * LANES] = blk.astype(BF16)


def _attention(q, k_ctx, v_ctx, k_lat=None, v_lat=None, sink=None, *, score_bound, window=False, tq=256, nsub=1):
    b, nq, _ = q.shape
    n_ctx = k_ctx.shape[1]
    has_lat = k_lat is not None
    has_sink = sink is not None
    rows = tq * nsub
    sink2 = sink.astype(F32) * LOG2E if has_sink else jnp.zeros((N_HEADS,), F32)
    shift = jnp.maximum(score_bound, jnp.max(sink2)) if has_sink else score_bound
    aux = jnp.concatenate([sink2, shift[None], jnp.zeros((1,), F32)])
    args = [aux, q, k_ctx, v_ctx]
    specs = [pl.BlockSpec(memory_space=pltpu.SMEM),
             pl.BlockSpec((1, rows, Q_WIDTH), lambda bi, i: (bi, i, 0)),
             pl.BlockSpec((1, n_ctx, KV_WIDTH), lambda bi, i: (bi, 0, 0)),
             pl.BlockSpec((1, n_ctx, KV_WIDTH), lambda bi, i: (bi, 0, 0))]
    if has_lat:
        n_lat = k_lat.shape[1]
        args += [k_lat, v_lat]
        specs += [pl.BlockSpec((1, n_lat, KV_WIDTH), lambda bi, i: (bi, 0, 0)),
                  pl.BlockSpec((1, n_lat, KV_WIDTH), lambda bi, i: (bi, 0, 0))]
    name = "window_attn" if window else ("global_attn" if has_lat else "ctx_attn")

    def call(bounded):
        return pl.pallas_call(
            functools.partial(_attn_kernel, tq=tq, nsub=nsub, has_lat=has_lat, window=window, has_sink=has_sink,
                              bounded=bounded, seq_len=nq),
            grid=(b, nq // rows),
            in_specs=specs,
            out_specs=pl.BlockSpec((1, rows, Q_WIDTH), lambda bi, i: (bi, i, 0)),
            out_shape=jax.ShapeDtypeStruct((b, nq, Q_WIDTH), BF16),
            compiler_params=_params(),
            name=name if bounded else name + "_rowmax",
        )(*args)

    return lax.cond(score_bound + shift <= MAX_EXP2_SPAN, lambda: call(True), lambda: call(False))


FF_CHUNK = 1024


def _mix_kernel(x_ref, mod_ref, g1_ref, g2_ref, oa_ref, ob_ref, oc_ref, wg_ref, wa_ref, wb_ref, wc_ref, wo_ref,
                w1_ref, w2_ref, out_ref):
    mod = mod_ref[0]
    x = x_ref[0]
    h = _modulate(x, g1_ref[...] * (1.0 + mod[1:2]), mod[0:1]).astype(BF16)
    y = None
    for k, (o_ref, w_ref) in enumerate(((oa_ref, wa_ref), (ob_ref, wb_ref), (oc_ref, wc_ref))):
        gate = jax.nn.sigmoid(_dot(h, wg_ref[:, k * D_MODEL:(k + 1) * D_MODEL]))
        term = gate * _dot(o_ref[0], w_ref[...])
        y = term if y is None else y + term
    x = x + mod[2:3] * _dot(y.astype(BF16), wo_ref[...])

    h = _modulate(x, g2_ref[...] * (1.0 + mod[4:5]), mod[3:4]).astype(BF16)
    acc = None
    for k in range(D_FF // FF_CHUNK):
        a = jnp.maximum(_dot(h, w1_ref[:, k * FF_CHUNK:(k + 1) * FF_CHUNK]), 0.0)
        part = _dot((a * a).astype(BF16), w2_ref[k * FF_CHUNK:(k + 1) * FF_CHUNK, :])
        acc = part if acc is None else acc + part
    out_ref[0] = x + mod[5:6] * acc


def _mix(x, mod, gain1, gain2, oa, ob, oc, w_gates, w_a, w_b, w_c, w_out, w1, w2, *, layer, tm):
    b, seq_len, _ = x.shape
    mod_map = (lambda bi, i: (bi, 0, 0)) if mod.shape[0] == b else (lambda bi, i: (0, 0, 0))
    tok = lambda width: pl.BlockSpec((1, tm, width), lambda bi, i: (bi, i, 0))
    return pl.pallas_call(
        _mix_kernel,
        grid=(b, seq_len // tm),
        in_specs=[tok(D_MODEL), pl.BlockSpec((1, 6, D_MODEL), mod_map), _const_spec((1, D_MODEL)),
                  _const_spec((1, D_MODEL)), tok(Q_WIDTH), tok(POOL_WIDTH), tok(Q_WIDTH),
                  _const_spec((D_MODEL, 3 * D_MODEL), layer), _const_spec((Q_WIDTH, D_MODEL), layer),
                  _const_spec((POOL_WIDTH, D_MODEL), layer), _const_spec((Q_WIDTH, D_MODEL), layer),
                  _const_spec((D_MODEL, D_MODEL), layer), _const_spec((D_MODEL, D_FF), layer),
                  _const_spec((D_FF, D_MODEL), layer)],
        out_specs=tok(D_MODEL),
        out_shape=jax.ShapeDtypeStruct(x.shape, F32),
        compiler_params=_params(),
        name="mix",
    )(x, mod, gain1, gain2, oa, ob, oc, w_gates, w_a, w_b, w_c, w_out, w1, w2)


def _rope_tables(n_tok):
    rows = n_tok // GRID_W
    r = jnp.repeat(jnp.arange(rows, dtype=F32), GRID_W)
    col = jnp.tile(jnp.arange(GRID_W, dtype=F32), rows)
    inv = 1.0 / (ROPE_THETA ** (jnp.arange(0, AXIS_DIM, 2, dtype=F32) / AXIS_DIM))
    ang = jnp.concatenate([r[:, None] * inv, col[:, None] * inv], axis=-1)
    pair = np.arange(LANES) % 32
    sign = np.where(np.arange(LANES) < LANES // 2, -1.0, 1.0).astype(np.float32)
    return jnp.cos(ang)[:, pair], jnp.sin(ang)[:, pair] * sign


def kernel(x, c, ctx, c_ctx, w_ada, b_ada, norm1, norm2, w_in, q_norm_a, k_norm_a, q_norm_c, k_norm_c, sink_c,
           w_pool, pool_scale, w_br_a, w_br_b, w_br_c, w_out, w_mlp1, w_mlp2):
    batch, seq_len, _ = x.shape
    n_ctx = ctx.shape[1]
    pad = (-(batch + 1)) % 8
    v = jnp.concatenate([c, c_ctx[None], jnp.zeros((pad, D_MODEL), F32)], axis=0)
    mods = _adaln(v, w_ada, b_ada)

    cos_lat, sin_lat = _rope_tables(seq_len)
    cos_ctx = jnp.ones((n_ctx, LANES), F32)
    sin_ctx = jnp.zeros((n_ctx, LANES), F32)
    q_scale = HEAD_DIM ** -0.5 * LOG2E

    w_proj = w_in[:, :, _PROJ_COLS].astype(BF16)
    w_gates = w_in[:, :, _GATES:].astype(BF16)
    wa = w_br_a[:, _OP].astype(BF16)
    wb = w_br_b.astype(BF16)
    wc = w_br_c[:, _OP].astype(BF16)
    wo = w_out.astype(BF16)
    w1 = w_mlp1.astype(BF16)
    w2 = w_mlp2.astype(BF16)

    xc = ctx
    for l in range(DEPTH):
        mod_lat = mods[l, :batch].reshape(batch, 6, D_MODEL)
        mod_ctx = mods[l, batch:batch + 1].reshape(1, 6, D_MODEL)
        head_gains = jnp.concatenate(
            [(q_norm_a[l] * q_scale)[_GP]] * N_GROUP + [(q_norm_c[l] * q_scale)[_GP]] * N_GROUP
            + [k_norm_a[l][_GP], k_norm_c[l][_GP]])[None]
        w_pool_bd = jax.scipy.linalg.block_diag(*[w_pool[l, g] for g in range(w_pool.shape[1])]).astype(BF16)
        ps = pool_scale[l][None]
        g1 = norm1[l][None]
        g2 = norm2[l][None]

        cqa, cqc, cka, ckc, cva, cvc, cob = _project(
            xc, mod_ctx, g1, w_proj, head_gains, cos_ctx, sin_ctx, w_pool_bd, ps, layer=l, tm=n_ctx)
        qa, qc, ka, kc, va, vc, ob = _project(
            x, mod_lat, g1, w_proj, head_gains, cos_lat, sin_lat, w_pool_bd, ps, layer=l, tm=1024, nsub=2)

        bound_a = _score_bound(q_norm_a[l] * q_scale, k_norm_a[l])
        bound_c = _score_bound(q_norm_c[l] * q_scale, k_norm_c[l])
        oa = _attention(qa, cka, cva, ka, va, score_bound=bound_a, nsub=2)
        oc = _attention(qc, ckc, cvc, kc, vc, sink_c[l], score_bound=bound_c, window=True, nsub=4)
        x = _mix(x, mod_lat, g1, g2, oa, ob, oc, w_gates, wa, wb, wc, wo, w1, w2, layer=l, tm=512)

        if l < DEPTH - 1:
            coa = _attention(cqa, cka, cva, score_bound=bound_a)
            coc = _attention(cqc, ckc, cvc, sink=sink_c[l], score_bound=bound_c)
            xc = _mix(xc, mod_ctx, g1, g2, coa, cob, coc, w_gates, wa, wb, wc, wo, w1, w2, layer=l, tm=n_ctx)
    return x
```

```python
import functools

import numpy as np
import jax
import jax.numpy as jnp
from jax import lax
from jax.experimental import pallas as pl
from jax.experimental.pallas import tpu as pltpu

D_MODEL = 1024
DEPTH = 2
GRID_W = 64
HEAD_DIM = 64
AXIS_DIM = HEAD_DIM // 2
ROPE_THETA = 10000.0
N_HEADS = 6
N_KV = 2
N_GROUP = N_HEADS // N_KV
Q_WIDTH = N_HEADS * HEAD_DIM
KV_WIDTH = N_KV * HEAD_DIM
POOL_WIDTH = 256
POOL_CH = 64
POOL_HALO = 8
WINDOW = 128
D_FF = 4 * D_MODEL
EPS = 1e-6
NEG = -1e30
LOG2E = 1.4426950408889634

LANES = 128
HALO_ROWS = 16
VMEM_LIMIT_BYTES = 56 * 1024 * 1024

F32 = jnp.float32
BF16 = jnp.bfloat16

_QA, _KA, _VA, _QC, _KC, _VC, _U, _GATES = 0, 384, 512, 640, 1024, 1152, 1280, 1536
PROJ_WIDTH = 2 * Q_WIDTH + 4 * KV_WIDTH + POOL_WIDTH


def _lane_parts(l):
    quarter, i = divmod(l, 32)
    return quarter % 2, quarter // 2, i


def _q_perm():
    idx = np.zeros(Q_WIDTH, np.int32)
    for c in range(N_GROUP):
        for l in range(LANES):
            slot, parity, i = _lane_parts(l)
            idx[c * LANES + l] = (c + N_GROUP * slot) * HEAD_DIM + 2 * i + parity
    return idx


def _k_perm():
    idx = np.zeros(KV_WIDTH, np.int32)
    for l in range(LANES):
        slot, parity, i = _lane_parts(l)
        idx[l] = slot * HEAD_DIM + 2 * i + parity
    return idx


def _gain_perm():
    return np.array([2 * _lane_parts(l)[2] + _lane_parts(l)[1] for l in range(LANES)], np.int32)


def _out_row_perm():
    idx = np.zeros(Q_WIDTH, np.int32)
    for c in range(N_GROUP):
        for half in range(2):
            for d in range(HEAD_DIM):
                idx[c * LANES + half * HEAD_DIM + d] = (c + N_GROUP * half) * HEAD_DIM + d
    return idx


_QP, _KP, _GP, _OP = _q_perm(), _k_perm(), _gain_perm(), _out_row_perm()
_PROJ_COLS = np.concatenate([
    _QA + _QP, _QC + _QP, _KA + _KP, _KC + _KP,
    np.arange(_VA, _VA + KV_WIDTH), np.arange(_VC, _VC + KV_WIDTH), np.arange(_U, _U + POOL_WIDTH)])
_HEAD_MEAN = np.array([[(a // LANES == b // LANES and _lane_parts(a % LANES)[0] == _lane_parts(b % LANES)[0])
                        / HEAD_DIM for b in range(2 * LANES)] for a in range(2 * LANES)], np.float32)
_ZQA, _ZQC, _ZKA, _ZKC, _ZVA, _ZVC, _ZU = 0, 384, 768, 896, 1024, 1152, 1280


def _const_spec(shape, layer=None):
    nd = len(shape)
    if layer is None:
        return pl.BlockSpec(shape, lambda *_: (0,) * nd, pipeline_mode=pl.Buffered(1))
    return pl.BlockSpec((None,) + shape, lambda *_: (layer,) + (0,) * nd, pipeline_mode=pl.Buffered(1))


def _params():
    return pltpu.CompilerParams(dimension_semantics=("arbitrary", "arbitrary"),
                                vmem_limit_bytes=VMEM_LIMIT_BYTES)


def _modulate(x, gain_scale, shift):
    ms = jnp.mean(x * x, axis=-1, keepdims=True)
    return x * lax.rsqrt(ms + EPS) * gain_scale + shift


def _dot(a, b):
    return jnp.dot(a, b, preferred_element_type=F32)


def _adaln_kernel(v_ref, w_ref, b_ref, o_ref):
    v = v_ref[...]
    s = (v * jax.nn.sigmoid(v)).astype(BF16)
    o_ref[0] = _dot(s, w_ref[0].astype(BF16)) + b_ref[0]


def _adaln(v, w_ada, b_ada):
    rows = v.shape[0]
    tn = 1536
    return pl.pallas_call(
        _adaln_kernel,
        grid=(DEPTH, 6 * D_MODEL // tn),
        in_specs=[pl.BlockSpec((rows, D_MODEL), lambda l, j: (0, 0)),
                  pl.BlockSpec((1, D_MODEL, tn), lambda l, j: (l, 0, j)),
                  pl.BlockSpec((1, 1, tn), lambda l, j: (l, 0, j))],
        out_specs=pl.BlockSpec((1, rows, tn), lambda l, j: (l, 0, j)),
        out_shape=jax.ShapeDtypeStruct((DEPTH, rows, 6 * D_MODEL), F32),
        compiler_params=_params(),
        name="adaln",
    )(v, w_ada, b_ada.reshape(DEPTH, 1, 6 * D_MODEL))


def _proj_kernel(x_ref, xp_ref, xn_ref, mod_ref, g_ref, w_ref, hg_ref, hm_ref, cos_ref, sin_ref, wp_ref, ps_ref,
                 qa_ref, qc_ref, ka_ref, kc_ref, va_ref, vc_ref, ob_ref, a0_ref, a1_ref, a2_ref, a3_ref,
                 *, tm, nsub, seq_len):
    i = pl.program_id(1)
    mod = mod_ref[0]
    shift = mod[0:1]
    gain_scale = g_ref[...] * (1.0 + mod[1:2])
    w = w_ref[...]
    hm = hm_ref[...]
    outs = ([(qa_ref, c) for c in range(N_GROUP)] + [(qc_ref, c) for c in range(N_GROUP)]
            + [(ka_ref, 0), (kc_ref, 0)])
    n0 = tm + 2 * POOL_HALO
    sub = tm // nsub
    def project_rows(t):
        h = _modulate(x_ref[0, t * sub:(t + 1) * sub], gain_scale, shift).astype(BF16)
        return _dot(h, w)

    zs = [project_rows(t) for t in range(nsub)]
    for t, z in enumerate(zs):
        rows = slice(t * sub, (t + 1) * sub)
        cos = cos_ref[rows]
        sin = sin_ref[rows]
        for p in range(4):
            zp = z[:, 2 * LANES * p:2 * LANES * (p + 1)]
            sq = zp * zp
            hi = sq.astype(BF16)
            lo = (sq - hi.astype(F32)).astype(BF16)
            ms = _dot(hi, hm) + _dot(lo, hm)
            y = zp * lax.rsqrt(ms + EPS) * hg_ref[:, 2 * LANES * p:2 * LANES * (p + 1)]
            for half in range(2):
                yc = y[:, half * LANES:(half + 1) * LANES]
                ref, c = outs[2 * p + half]
                ref[0, rows, c * LANES:(c + 1) * LANES] = (
                    yc * cos + pltpu.roll(yc, LANES // 2, 1) * sin).astype(BF16)
        va_ref[0, rows] = z[:, _ZVA:_ZVA + LANES].astype(BF16)
        vc_ref[0, rows] = z[:, _ZVC:_ZVC + LANES].astype(BF16)
        a0_ref[POOL_HALO + t * sub:POOL_HALO + (t + 1) * sub] = z[:, _ZU:_ZU + POOL_WIDTH]

    wu = w[:, _ZU:_ZU + POOL_WIDTH]

    def halo(ref, valid):
        hh = _modulate(ref[0], gain_scale, shift).astype(BF16)
        return jnp.where(valid, _dot(hh, wu), 0.0)

    def put(ref, val):
        ref[0:n0] = val
        ref[n0:n0 + POOL_HALO] = jnp.zeros((POOL_HALO, val.shape[1]), F32)

    a0_ref[0:POOL_HALO] = halo(xp_ref, i > 0)[HALO_ROWS - POOL_HALO:]
    a0_ref[POOL_HALO + tm:n0] = halo(xn_ref, i < pl.num_programs(1) - 1)[:POOL_HALO]
    u = a0_ref[POOL_HALO:POOL_HALO + tm]
    a0_ref[n0:n0 + POOL_HALO] = jnp.zeros((POOL_HALO, POOL_WIDTH), F32)
    w2 = a0_ref[0:n0] + a0_ref[1:n0 + 1]
    put(a1_ref, w2)
    w4 = w2 + a1_ref[2:n0 + 2]
    put(a2_ref, w4)
    w8 = w4[:, LANES:] + a2_ref[4:n0 + 4, LANES:]
    put(a3_ref, w8)
    w16 = w8 + a3_ref[8:n0 + 8]
    lane = lax.broadcasted_iota(jnp.int32, (1, LANES), 1)
    first = lane < POOL_CH
    t = i * tm + lax.broadcasted_iota(jnp.int32, (tm, 1), 0)
    pooled = []
    for col, (narrow, wide) in enumerate(((a1_ref[7:7 + tm, :LANES], a2_ref[6:6 + tm, :LANES]),
                                          (a3_ref[4:4 + tm], w16[0:tm]))):
        half = jnp.where(first, 1, 2) << (2 * col)
        cnt = (jnp.minimum(t + half, seq_len) - jnp.maximum(t - half, 0)).astype(F32)
        pooled.append(jnp.where(first, narrow, wide) / cnt - u[:, col * LANES:(col + 1) * LANES])
    pooled = jnp.concatenate(pooled, axis=1)
    ob_ref[0] = (_dot(pooled.astype(BF16), wp_ref[...]) * ps_ref[...]).astype(BF16)


def _project(x, mod, gain, w_proj, head_gains, cos_tab, sin_tab, w_pool_bd, pool_scale, *, layer, tm, nsub=1):
    b, seq_len, _ = x.shape
    nb = seq_len // tm
    hb = tm // HALO_ROWS
    mod_map = (lambda bi, i: (bi, 0, 0)) if mod.shape[0] == b else (lambda bi, i: (0, 0, 0))
    tok = lambda width: pl.BlockSpec((1, tm, width), lambda bi, i: (bi, i, 0))
    out = lambda width: jax.ShapeDtypeStruct((b, seq_len, width), BF16)
    return pl.pallas_call(
        functools.partial(_proj_kernel, tm=tm, nsub=nsub, seq_len=seq_len),
        grid=(b, nb),
        in_specs=[tok(D_MODEL),
                  pl.BlockSpec((1, HALO_ROWS, D_MODEL), lambda bi, i: (bi, jnp.maximum(i * hb - 1, 0), 0)),
                  pl.BlockSpec((1, HALO_ROWS, D_MODEL),
                               lambda bi, i: (bi, jnp.minimum((i + 1) * hb, nb * hb - 1), 0)),
                  pl.BlockSpec((1, 6, D_MODEL), mod_map),
                  _const_spec((1, D_MODEL)),
                  _const_spec((D_MODEL, PROJ_WIDTH), layer),
                  _const_spec((1, 8 * LANES)),
                  _const_spec((2 * LANES, 2 * LANES)),
                  pl.BlockSpec((tm, LANES), lambda bi, i: (i, 0)),
                  pl.BlockSpec((tm, LANES), lambda bi, i: (i, 0)),
                  _const_spec((POOL_WIDTH, POOL_WIDTH)),
                  _const_spec((1, POOL_WIDTH))],
        out_specs=[tok(Q_WIDTH), tok(Q_WIDTH), tok(KV_WIDTH), tok(KV_WIDTH), tok(KV_WIDTH), tok(KV_WIDTH),
                   tok(POOL_WIDTH)],
        out_shape=[out(Q_WIDTH), out(Q_WIDTH), out(KV_WIDTH), out(KV_WIDTH), out(KV_WIDTH), out(KV_WIDTH),
                   out(POOL_WIDTH)],
        scratch_shapes=[pltpu.VMEM((tm + 3 * POOL_HALO, POOL_WIDTH), F32)] * 3
                       + [pltpu.VMEM((tm + 3 * POOL_HALO, LANES), F32)],
        compiler_params=_params(),
        name="project",
    )(x, x, x, mod, gain, w_proj, head_gains, jnp.asarray(_HEAD_MEAN, BF16), cos_tab, sin_tab, w_pool_bd,
      pool_scale)


_NT = (((1,), (1,)), ((), ()))
SKEW = 1
MAX_EXP2_SPAN = 80.0
NORM_MARGIN = 1.02


def _score_bound(q_gain, k_gain):
    return HEAD_DIM * jnp.max(jnp.abs(q_gain)) * jnp.max(jnp.abs(k_gain)) * NORM_MARGIN


def _attn_kernel(aux_ref, *refs, tq, nsub, has_lat, window, has_sink, bounded, seq_len):
    q_ref, kc_ref, vc_ref = refs[:3]
    if has_lat:
        kl_ref, vl_ref = refs[3:5]
    o_ref = refs[-1]
    i = pl.program_id(1)

    lane = lax.broadcasted_iota(jnp.int32, (1, LANES), 1)
    slot0 = ((lane >> 5) & 1) == 0
    zero = jnp.zeros((), BF16)
    kc = kc_ref[0]
    vc = vc_ref[0]
    lat = []
    for t in range(nsub if has_lat else 0):
        if window:
            nk = tq + 2 * WINDOW
            q0 = (i * nsub + t) * tq
            start = pl.multiple_of(jnp.clip(q0 - WINDOW, 0, seq_len - nk), WINDOW)
            r = lax.broadcasted_iota(jnp.int32, (tq, nk), 0)
            c = lax.broadcasted_iota(jnp.int32, (tq, nk), 1)
            lat.append((kl_ref[0, pl.ds(start, nk), :], vl_ref[0, pl.ds(start, nk), :],
                        jnp.where(jnp.abs(q0 - start + r - c) <= WINDOW, 0.0, NEG)))
        else:
            lat.append((kl_ref[0], vl_ref[0], None))

    def scores(t, j):
        kv, g = divmod(j, N_GROUP)
        qj = jnp.where(slot0 if kv == 0 else ~slot0, q_ref[0, t * tq:(t + 1) * tq, g * LANES:(g + 1) * LANES], zero)
        s_ctx = lax.dot_general(qj, kc, _NT, preferred_element_type=F32)
        m = aux_ref[6] if bounded else jnp.max(s_ctx, axis=-1, keepdims=True)
        s_lat = None
        if has_lat:
            kl, _, mask_bias = lat[t]
            s_lat = lax.dot_general(qj, kl, _NT, preferred_element_type=F32)
            if window:
                s_lat = s_lat + mask_bias
            if not bounded:
                m = jnp.maximum(m, jnp.max(s_lat, axis=-1, keepdims=True))
        return s_ctx, s_lat, m

    def weighted_values(t, j, s_ctx, s_lat, m):
        if has_sink and not bounded:
            m = jnp.maximum(m, aux_ref[j])
        p_ctx = jnp.exp2(s_ctx - m)
        denom = jnp.sum(p_ctx, axis=-1, keepdims=True)
        o = _dot(p_ctx.astype(BF16), vc)
        if has_lat:
            p_lat = jnp.exp2(s_lat - m)
            denom = denom + jnp.sum(p_lat, axis=-1, keepdims=True)
            o = o + _dot(p_lat.astype(BF16), lat[t][1])
        if has_sink:
            denom = denom + jnp.exp2(jnp.full((1, 1), aux_ref[j], F32) - m)
        return o / denom

    units = [(t, j) for t in range(nsub) for j in range(N_HEADS)]
    pending = [scores(*u) for u in units[:SKEW]]
    outs = {}
    for n, u in enumerate(units):
        if n + SKEW < len(units):
            pending.append(scores(*units[n + SKEW]))
        outs[u] = weighted_values(*u, *pending.pop(0))
    for t in range(nsub):
        for g in range(N_GROUP):
            blk = jnp.where(lane < HEAD_DIM, outs[t, g], outs[t, N_GROUP + g])
            o_ref[0, t * tq:(t + 1) * tq, g * LANES:(g + 1) * LANES] = blk.astype(BF16)


def _attention(q, k_ctx, v_ctx, k_lat=None, v_lat=None, sink=None, *, score_bound, window=False, tq=256, nsub=1):
    b, nq, _ = q.shape
    n_ctx = k_ctx.shape[1]
    has_lat = k_lat is not None
    has_sink = sink is not None
    rows = tq * nsub
    sink2 = sink.astype(F32) * LOG2E if has_sink else jnp.zeros((N_HEADS,), F32)
    shift = jnp.maximum(score_bound, jnp.max(sink2)) if has_sink else score_bound
    aux = jnp.concatenate([sink2, shift[None], jnp.zeros((1,), F32)])
    args = [aux, q, k_ctx, v_ctx]
    specs = [pl.BlockSpec(memory_space=pltpu.SMEM),
             pl.BlockSpec((1, rows, Q_WIDTH), lambda bi, i: (bi, i, 0)),
             pl.BlockSpec((1, n_ctx, KV_WIDTH), lambda bi, i: (bi, 0, 0)),
             pl.BlockSpec((1, n_ctx, KV_WIDTH), lambda bi, i: (bi, 0, 0))]
    if has_lat:
        n_lat = k_lat.shape[1]
        args += [k_lat, v_lat]
        specs += [pl.BlockSpec((1, n_lat, KV_WIDTH), lambda bi, i: (bi, 0, 0)),
                  pl.BlockSpec((1, n_lat, KV_WIDTH), lambda bi, i: (bi, 0, 0))]
    name = "window_attn" if window else ("global_attn" if has_lat else "ctx_attn")

    def call(bounded):
        return pl.pallas_call(
            functools.partial(_attn_kernel, tq=tq, nsub=nsub, has_lat=has_lat, window=window, has_sink=has_sink,
                              bounded=bounded, seq_len=nq),
            grid=(b, nq // rows),
            in_specs=specs,
            out_specs=pl.BlockSpec((1, rows, Q_WIDTH), lambda bi, i: (bi, i, 0)),
            out_shape=jax.ShapeDtypeStruct((b, nq, Q_WIDTH), BF16),
            compiler_params=_params(),
            name=name if bounded else name + "_rowmax",
        )(*args)

    return lax.cond(score_bound + shift <= MAX_EXP2_SPAN, lambda: call(True), lambda: call(False))


FF_CHUNK = 1024


def _mix_kernel(x_ref, mod_ref, g1_ref, g2_ref, oa_ref, ob_ref, oc_ref, wg_ref, wa_ref, wb_ref, wc_ref, wo_ref,
                w1_ref, w2_ref, out_ref, *, nsub):
    mod = mod_ref[0]
    sub = x_ref.shape[1] // nsub

    def merge(t):
        rows = slice(t * sub, (t + 1) * sub)
        x = x_ref[0, rows]
        h = _modulate(x, g1_ref[...] * (1.0 + mod[1:2]), mod[0:1]).astype(BF16)
        y = None
        for k, (o_ref, w_ref) in enumerate(((oa_ref, wa_ref), (ob_ref, wb_ref), (oc_ref, wc_ref))):
            gate = jax.nn.sigmoid(_dot(h, wg_ref[:, k * D_MODEL:(k + 1) * D_MODEL]))
            term = gate * _dot(o_ref[0, rows], w_ref[...])
            y = term if y is None else y + term
        return x + mod[2:3] * _dot(y.astype(BF16), wo_ref[...])

    def mlp(t, x):
        h = _modulate(x, g2_ref[...] * (1.0 + mod[4:5]), mod[3:4]).astype(BF16)
        acc = None
        for k in range(D_FF // FF_CHUNK):
            a = jnp.maximum(_dot(h, w1_ref[:, k * FF_CHUNK:(k + 1) * FF_CHUNK]), 0.0)
            part = _dot((a * a).astype(BF16), w2_ref[k * FF_CHUNK:(k + 1) * FF_CHUNK, :])
            acc = part if acc is None else acc + part
        out_ref[0, t * sub:(t + 1) * sub] = x + mod[5:6] * acc

    merged = [merge(t) for t in range(nsub)]
    for t, x in enumerate(merged):
        mlp(t, x)


def _mix(x, mod, gain1, gain2, oa, ob, oc, w_gates, w_a, w_b, w_c, w_out, w1, w2, *, layer, tm, nsub=1):
    b, seq_len, _ = x.shape
    mod_map = (lambda bi, i: (bi, 0, 0)) if mod.shape[0] == b else (lambda bi, i: (0, 0, 0))
    tok = lambda width: pl.BlockSpec((1, tm, width), lambda bi, i: (bi, i, 0))
    return pl.pallas_call(
        functools.partial(_mix_kernel, nsub=nsub),
        grid=(b, seq_len // tm),
        in_specs=[tok(D_MODEL), pl.BlockSpec((1, 6, D_MODEL), mod_map), _const_spec((1, D_MODEL)),
                  _const_spec((1, D_MODEL)), tok(Q_WIDTH), tok(POOL_WIDTH), tok(Q_WIDTH),
                  _const_spec((D_MODEL, 3 * D_MODEL), layer), _const_spec((Q_WIDTH, D_MODEL), layer),
                  _const_spec((POOL_WIDTH, D_MODEL), layer), _const_spec((Q_WIDTH, D_MODEL), layer),
                  _const_spec((D_MODEL, D_MODEL), layer), _const_spec((D_MODEL, D_FF), layer),
                  _const_spec((D_FF, D_MODEL), layer)],
        out_specs=tok(D_MODEL),
        out_shape=jax.ShapeDtypeStruct(x.shape, F32),
        compiler_params=_params(),
        name="mix",
    )(x, mod, gain1, gain2, oa, ob, oc, w_gates, w_a, w_b, w_c, w_out, w1, w2)


def _rope_tables(n_tok):
    rows = n_tok // GRID_W
    r = jnp.repeat(jnp.arange(rows, dtype=F32), GRID_W)
    col = jnp.tile(jnp.arange(GRID_W, dtype=F32), rows)
    inv = 1.0 / (ROPE_THETA ** (jnp.arange(0, AXIS_DIM, 2, dtype=F32) / AXIS_DIM))
    ang = jnp.concatenate([r[:, None] * inv, col[:, None] * inv], axis=-1)
    pair = np.arange(LANES) % 32
    sign = np.where(np.arange(LANES) < LANES // 2, -1.0, 1.0).astype(np.float32)
    return jnp.cos(ang)[:, pair], jnp.sin(ang)[:, pair] * sign


def kernel(x, c, ctx, c_ctx, w_ada, b_ada, norm1, norm2, w_in, q_norm_a, k_norm_a, q_norm_c, k_norm_c, sink_c,
           w_pool, pool_scale, w_br_a, w_br_b, w_br_c, w_out, w_mlp1, w_mlp2):
    batch, seq_len, _ = x.shape
    n_ctx = ctx.shape[1]
    pad = (-(batch + 1)) % 8
    v = jnp.concatenate([c, c_ctx[None], jnp.zeros((pad, D_MODEL), F32)], axis=0)
    mods = _adaln(v, w_ada, b_ada)

    cos_lat, sin_lat = _rope_tables(seq_len)
    cos_ctx = jnp.ones((n_ctx, LANES), F32)
    sin_ctx = jnp.zeros((n_ctx, LANES), F32)
    q_scale = HEAD_DIM ** -0.5 * LOG2E

    per_layer = lambda f: jnp.stack([f(l) for l in range(DEPTH)])
    w_proj = per_layer(lambda l: w_in[l][:, _PROJ_COLS].astype(BF16))
    w_gates = w_in[:, :, _GATES:].astype(BF16)
    wa = per_layer(lambda l: w_br_a[l][_OP].astype(BF16))
    wb = w_br_b.astype(BF16)
    wc = per_layer(lambda l: w_br_c[l][_OP].astype(BF16))
    wo = w_out.astype(BF16)
    w1 = w_mlp1.astype(BF16)
    w2 = w_mlp2.astype(BF16)

    xc = ctx
    for l in range(DEPTH):
        mod_lat = mods[l, :batch].reshape(batch, 6, D_MODEL)
        mod_ctx = mods[l, batch:batch + 1].reshape(1, 6, D_MODEL)
        head_gains = jnp.concatenate(
            [(q_norm_a[l] * q_scale)[_GP]] * N_GROUP + [(q_norm_c[l] * q_scale)[_GP]] * N_GROUP
            + [k_norm_a[l][_GP], k_norm_c[l][_GP]])[None]
        w_pool_bd = jax.scipy.linalg.block_diag(*[w_pool[l, g] for g in range(w_pool.shape[1])]).astype(BF16)
        ps = pool_scale[l][None]
        g1 = norm1[l][None]
        g2 = norm2[l][None]

        cqa, cqc, cka, ckc, cva, cvc, cob = _project(
            xc, mod_ctx, g1, w_proj, head_gains, cos_ctx, sin_ctx, w_pool_bd, ps, layer=l, tm=n_ctx)
        qa, qc, ka, kc, va, vc, ob = _project(
            x, mod_lat, g1, w_proj, head_gains, cos_lat, sin_lat, w_pool_bd, ps, layer=l, tm=1024, nsub=2)

        bound_a = _score_bound(q_norm_a[l] * q_scale, k_norm_a[l])
        bound_c = _score_bound(q_norm_c[l] * q_scale, k_norm_c[l])
        oa = _attention(qa, cka, cva, ka, va, score_bound=bound_a, nsub=4)
        oc = _attention(qc, ckc, cvc, kc, vc, sink_c[l], score_bound=bound_c, window=True, nsub=4)
        x = _mix(x, mod_lat, g1, g2, oa, ob, oc, w_gates, wa, wb, wc, wo, w1, w2, layer=l, tm=512, nsub=2)

        if l < DEPTH - 1:
            coa = _attention(cqa, cka, cva, score_bound=bound_a)
            coc = _attention(cqc, ckc, cvc, sink=sink_c[l], score_bound=bound_c)
            xc = _mix(xc, mod_ctx, g1, g2, coa, cob, coc, w_gates, wa, wb, wc, wo, w1, w2, layer=l, tm=n_ctx)
    return x
```

```python
import functools

import numpy as np
import jax
import jax.numpy as jnp
from jax import lax
from jax.experimental import pallas as pl
from jax.experimental.pallas import tpu as pltpu

D_MODEL = 1024
DEPTH = 2
GRID_W = 64
HEAD_DIM = 64
AXIS_DIM = HEAD_DIM // 2
ROPE_THETA = 10000.0
N_HEADS = 6
N_KV = 2
N_GROUP = N_HEADS // N_KV
Q_WIDTH = N_HEADS * HEAD_DIM
KV_WIDTH = N_KV * HEAD_DIM
POOL_WIDTH = 256
POOL_CH = 64
POOL_HALO = 8
WINDOW = 128
D_FF = 4 * D_MODEL
EPS = 1e-6
NEG = -1e30
LOG2E = 1.4426950408889634

LANES = 128
HALO_ROWS = 16
VMEM_LIMIT_BYTES = 56 * 1024 * 1024

F32 = jnp.float32
BF16 = jnp.bfloat16

_QA, _KA, _VA, _QC, _KC, _VC, _U, _GATES = 0, 384, 512, 640, 1024, 1152, 1280, 1536
PROJ_WIDTH = 2 * Q_WIDTH + 4 * KV_WIDTH + POOL_WIDTH


def _lane_parts(l):
    quarter, i = divmod(l, 32)
    return quarter % 2, quarter // 2, i


def _q_perm():
    idx = np.zeros(Q_WIDTH, np.int32)
    for c in range(N_GROUP):
        for l in range(LANES):
            slot, parity, i = _lane_parts(l)
            idx[c * LANES + l] = (c + N_GROUP * slot) * HEAD_DIM + 2 * i + parity
    return idx


def _k_perm():
    idx = np.zeros(KV_WIDTH, np.int32)
    for l in range(LANES):
        slot, parity, i = _lane_parts(l)
        idx[l] = slot * HEAD_DIM + 2 * i + parity
    return idx


def _gain_perm():
    return np.array([2 * _lane_parts(l)[2] + _lane_parts(l)[1] for l in range(LANES)], np.int32)


def _out_row_perm():
    idx = np.zeros(Q_WIDTH, np.int32)
    for c in range(N_GROUP):
        for half in range(2):
            for d in range(HEAD_DIM):
                idx[c * LANES + half * HEAD_DIM + d] = (c + N_GROUP * half) * HEAD_DIM + d
    return idx


_QP, _KP, _GP, _OP = _q_perm(), _k_perm(), _gain_perm(), _out_row_perm()
_PROJ_COLS = np.concatenate([
    _QA + _QP, _QC + _QP, _KA + _KP, _KC + _KP,
    np.arange(_VA, _VA + KV_WIDTH), np.arange(_VC, _VC + KV_WIDTH), np.arange(_U, _U + POOL_WIDTH)])
_HEAD_MEAN = np.array([[(a // LANES == b // LANES and _lane_parts(a % LANES)[0] == _lane_parts(b % LANES)[0])
                        / HEAD_DIM for b in range(2 * LANES)] for a in range(2 * LANES)], np.float32)
_ZQA, _ZQC, _ZKA, _ZKC, _ZVA, _ZVC, _ZU = 0, 384, 768, 896, 1024, 1152, 1280


def _const_spec(shape, layer=None):
    nd = len(shape)
    if layer is None:
        return pl.BlockSpec(shape, lambda *_: (0,) * nd, pipeline_mode=pl.Buffered(1))
    return pl.BlockSpec((None,) + shape, lambda *_: (layer,) + (0,) * nd, pipeline_mode=pl.Buffered(1))


def _params():
    return pltpu.CompilerParams(dimension_semantics=("arbitrary", "arbitrary"),
                                vmem_limit_bytes=VMEM_LIMIT_BYTES)


def _modulate(x, gain_scale, shift):
    ms = jnp.mean(x * x, axis=-1, keepdims=True)
    return x * lax.rsqrt(ms + EPS) * gain_scale + shift


def _dot(a, b):
    return jnp.dot(a, b, preferred_element_type=F32)


def _adaln_kernel(v_ref, w_ref, b_ref, o_ref):
    v = v_ref[...]
    s = (v * jax.nn.sigmoid(v)).astype(BF16)
    o_ref[0] = _dot(s, w_ref[0].astype(BF16)) + b_ref[0]


def _adaln(v, w_ada, b_ada):
    rows = v.shape[0]
    tn = 1536
    return pl.pallas_call(
        _adaln_kernel,
        grid=(DEPTH, 6 * D_MODEL // tn),
        in_specs=[pl.BlockSpec((rows, D_MODEL), lambda l, j: (0, 0)),
                  pl.BlockSpec((1, D_MODEL, tn), lambda l, j: (l, 0, j)),
                  pl.BlockSpec((1, 1, tn), lambda l, j: (l, 0, j))],
        out_specs=pl.BlockSpec((1, rows, tn), lambda l, j: (l, 0, j)),
        out_shape=jax.ShapeDtypeStruct((DEPTH, rows, 6 * D_MODEL), F32),
        compiler_params=_params(),
        name="adaln",
    )(v, w_ada, b_ada.reshape(DEPTH, 1, 6 * D_MODEL))


def _proj_kernel(x_ref, xp_ref, xn_ref, mod_ref, g_ref, w_ref, hg_ref, hm_ref, cos_ref, sin_ref, wp_ref, ps_ref,
                 qa_ref, qc_ref, ka_ref, kc_ref, va_ref, vc_ref, ob_ref, a0_ref, a1_ref, a2_ref, a3_ref,
                 *, tm, nsub, seq_len):
    i = pl.program_id(1)
    mod = mod_ref[0]
    shift = mod[0:1]
    gain_scale = g_ref[...] * (1.0 + mod[1:2])
    hm = hm_ref[...]
    outs = ([(qa_ref, c) for c in range(N_GROUP)] + [(qc_ref, c) for c in range(N_GROUP)]
            + [(ka_ref, 0), (kc_ref, 0)])
    n0 = tm + 2 * POOL_HALO
    sub = tm // nsub
    hs = [_modulate(x_ref[0, t * sub:(t + 1) * sub], gain_scale, shift).astype(BF16) for t in range(nsub)]
    slab = 2 * LANES

    def project_slab(t, p):
        return _dot(hs[t], w_ref[:, slab * p:slab * (p + 1)])

    def finish_slab(t, p, zp):
        rows = slice(t * sub, (t + 1) * sub)
        if slab * p == _ZVA:
            va_ref[0, rows] = zp[:, :LANES].astype(BF16)
            vc_ref[0, rows] = zp[:, LANES:].astype(BF16)
        elif slab * p == _ZU:
            a0_ref[POOL_HALO + t * sub:POOL_HALO + (t + 1) * sub] = zp
        else:
            sq = zp * zp
            hi = sq.astype(BF16)
            lo = (sq - hi.astype(F32)).astype(BF16)
            ms = _dot(hi, hm) + _dot(lo, hm)
            y = zp * lax.rsqrt(ms + EPS) * hg_ref[:, slab * p:slab * (p + 1)]
            for half in range(2):
                yc = y[:, half * LANES:(half + 1) * LANES]
                ref, c = outs[2 * p + half]
                ref[0, rows, c * LANES:(c + 1) * LANES] = (
                    yc * cos_ref[rows] + pltpu.roll(yc, LANES // 2, 1) * sin_ref[rows]).astype(BF16)

    def run(units):
        pending = [project_slab(*units[0])]
        for n, u in enumerate(units):
            if n + 1 < len(units):
                pending.append(project_slab(*units[n + 1]))
            finish_slab(*u, pending.pop(0))

    n_slab = PROJ_WIDTH // slab
    run([(t, p) for t in range(nsub) for p in [n_slab - 1] + list(range(n_slab - 1))])

    wu = w_ref[:, _ZU:_ZU + POOL_WIDTH]

    def halo(ref, valid):
        hh = _modulate(ref[0], gain_scale, shift).astype(BF16)
        return jnp.where(valid, _dot(hh, wu), 0.0)

    def put(ref, val):
        ref[0:n0] = val
        ref[n0:n0 + POOL_HALO] = jnp.zeros((POOL_HALO, val.shape[1]), F32)

    a0_ref[0:POOL_HALO] = halo(xp_ref, i > 0)[HALO_ROWS - POOL_HALO:]
    a0_ref[POOL_HALO + tm:n0] = halo(xn_ref, i < pl.num_programs(1) - 1)[:POOL_HALO]
    u = a0_ref[POOL_HALO:POOL_HALO + tm]
    a0_ref[n0:n0 + POOL_HALO] = jnp.zeros((POOL_HALO, POOL_WIDTH), F32)
    w2 = a0_ref[0:n0] + a0_ref[1:n0 + 1]
    put(a1_ref, w2)
    w4 = w2 + a1_ref[2:n0 + 2]
    put(a2_ref, w4)
    w8 = w4[:, LANES:] + a2_ref[4:n0 + 4, LANES:]
    put(a3_ref, w8)
    w16 = w8 + a3_ref[8:n0 + 8]
    lane = lax.broadcasted_iota(jnp.int32, (1, LANES), 1)
    first = lane < POOL_CH
    t = i * tm + lax.broadcasted_iota(jnp.int32, (tm, 1), 0)
    pooled = []
    for col, (narrow, wide) in enumerate(((a1_ref[7:7 + tm, :LANES], a2_ref[6:6 + tm, :LANES]),
                                          (a3_ref[4:4 + tm], w16[0:tm]))):
        half = jnp.where(first, 1, 2) << (2 * col)
        cnt = (jnp.minimum(t + half, seq_len) - jnp.maximum(t - half, 0)).astype(F32)
        pooled.append(jnp.where(first, narrow, wide) / cnt - u[:, col * LANES:(col + 1) * LANES])
    pooled = jnp.concatenate(pooled, axis=1)
    ob_ref[0] = (_dot(pooled.astype(BF16), wp_ref[...]) * ps_ref[...]).astype(BF16)


def _project(x, mod, gain, w_proj, head_gains, cos_tab, sin_tab, w_pool_bd, pool_scale, *, layer, tm, nsub=1):
    b, seq_len, _ = x.shape
    nb = seq_len // tm
    hb = tm // HALO_ROWS
    mod_map = (lambda bi, i: (bi, 0, 0)) if mod.shape[0] == b else (lambda bi, i: (0, 0, 0))
    tok = lambda width: pl.BlockSpec((1, tm, width), lambda bi, i: (bi, i, 0))
    out = lambda width: jax.ShapeDtypeStruct((b, seq_len, width), BF16)
    return pl.pallas_call(
        functools.partial(_proj_kernel, tm=tm, nsub=nsub, seq_len=seq_len),
        grid=(b, nb),
        in_specs=[tok(D_MODEL),
                  pl.BlockSpec((1, HALO_ROWS, D_MODEL), lambda bi, i: (bi, jnp.maximum(i * hb - 1, 0), 0)),
                  pl.BlockSpec((1, HALO_ROWS, D_MODEL),
                               lambda bi, i: (bi, jnp.minimum((i + 1) * hb, nb * hb - 1), 0)),
                  pl.BlockSpec((1, 6, D_MODEL), mod_map),
                  _const_spec((1, D_MODEL)),
                  _const_spec((D_MODEL, PROJ_WIDTH), layer),
                  _const_spec((1, 8 * LANES)),
                  _const_spec((2 * LANES, 2 * LANES)),
                  pl.BlockSpec((tm, LANES), lambda bi, i: (i, 0)),
                  pl.BlockSpec((tm, LANES), lambda bi, i: (i, 0)),
                  _const_spec((POOL_WIDTH, POOL_WIDTH)),
                  _const_spec((1, POOL_WIDTH))],
        out_specs=[tok(Q_WIDTH), tok(Q_WIDTH), tok(KV_WIDTH), tok(KV_WIDTH), tok(KV_WIDTH), tok(KV_WIDTH),
                   tok(POOL_WIDTH)],
        out_shape=[out(Q_WIDTH), out(Q_WIDTH), out(KV_WIDTH), out(KV_WIDTH), out(KV_WIDTH), out(KV_WIDTH),
                   out(POOL_WIDTH)],
        scratch_shapes=[pltpu.VMEM((tm + 3 * POOL_HALO, POOL_WIDTH), F32)] * 3
                       + [pltpu.VMEM((tm + 3 * POOL_HALO, LANES), F32)],
        compiler_params=_params(),
        name="project",
    )(x, x, x, mod, gain, w_proj, head_gains, jnp.asarray(_HEAD_MEAN, BF16), cos_tab, sin_tab, w_pool_bd,
      pool_scale)


_NT = (((1,), (1,)), ((), ()))
SKEW = 1
MAX_EXP2_SPAN = 80.0
NORM_MARGIN = 1.02


def _score_bound(q_gain, k_gain):
    return HEAD_DIM * jnp.max(jnp.abs(q_gain)) * jnp.max(jnp.abs(k_gain)) * NORM_MARGIN


def _attn_kernel(aux_ref, *refs, tq, nsub, has_lat, window, has_sink, bounded, seq_len):
    q_ref, kc_ref, vc_ref = refs[:3]
    if has_lat:
        kl_ref, vl_ref = refs[3:5]
    o_ref = refs[-1]
    i = pl.program_id(1)

    lane = lax.broadcasted_iota(jnp.int32, (1, LANES), 1)
    slot0 = ((lane >> 5) & 1) == 0
    zero = jnp.zeros((), BF16)
    kc = kc_ref[0]
    vc = vc_ref[0]
    lat = []
    for t in range(nsub if has_lat else 0):
        if window:
            nk = tq + 2 * WINDOW
            q0 = (i * nsub + t) * tq
            start = pl.multiple_of(jnp.clip(q0 - WINDOW, 0, seq_len - nk), WINDOW)
            r = lax.broadcasted_iota(jnp.int32, (tq, nk), 0)
            c = lax.broadcasted_iota(jnp.int32, (tq, nk), 1)
            lat.append((kl_ref[0, pl.ds(start, nk), :], vl_ref[0, pl.ds(start, nk), :],
                        jnp.where(jnp.abs(q0 - start + r - c) <= WINDOW, 0.0, NEG)))
        else:
            lat.append((kl_ref[0], vl_ref[0], None))

    def scores(t, j):
        kv, g = divmod(j, N_GROUP)
        qj = jnp.where(slot0 if kv == 0 else ~slot0, q_ref[0, t * tq:(t + 1) * tq, g * LANES:(g + 1) * LANES], zero)
        s_ctx = lax.dot_general(qj, kc, _NT, preferred_element_type=F32)
        m = aux_ref[6] if bounded else jnp.max(s_ctx, axis=-1, keepdims=True)
        s_lat = None
        if has_lat:
            kl, _, mask_bias = lat[t]
            s_lat = lax.dot_general(qj, kl, _NT, preferred_element_type=F32)
            if window:
                s_lat = s_lat + mask_bias
            if not bounded:
                m = jnp.maximum(m, jnp.max(s_lat, axis=-1, keepdims=True))
        return s_ctx, s_lat, m

    def weighted_values(t, j, s_ctx, s_lat, m):
        if has_sink and not bounded:
            m = jnp.maximum(m, aux_ref[j])
        p_ctx = jnp.exp2(s_ctx - m)
        denom = jnp.sum(p_ctx, axis=-1, keepdims=True)
        o = _dot(p_ctx.astype(BF16), vc)
        if has_lat:
            p_lat = jnp.exp2(s_lat - m)
            denom = denom + jnp.sum(p_lat, axis=-1, keepdims=True)
            o = o + _dot(p_lat.astype(BF16), lat[t][1])
        if has_sink:
            denom = denom + jnp.exp2(jnp.full((1, 1), aux_ref[j], F32) - m)
        return o / denom

    units = [(t, j) for t in range(nsub) for j in range(N_HEADS)]
    pending = [scores(*u) for u in units[:SKEW]]
    outs = {}
    for n, u in enumerate(units):
        if n + SKEW < len(units):
            pending.append(scores(*units[n + SKEW]))
        outs[u] = weighted_values(*u, *pending.pop(0))
    for t in range(nsub):
        for g in range(N_GROUP):
            blk = jnp.where(lane < HEAD_DIM, outs[t, g], outs[t, N_GROUP + g])
            o_ref[0, t * tq:(t + 1) * tq, g * LANES:(g + 1) * LANES] = blk.astype(BF16)


def _attention(q, k_ctx, v_ctx, k_lat=None, v_lat=None, sink=None, *, score_bound, window=False, tq=256, nsub=1):
    b, nq, _ = q.shape
    n_ctx = k_ctx.shape[1]
    has_lat = k_lat is not None
    has_sink = sink is not None
    rows = tq * nsub
    sink2 = sink.astype(F32) * LOG2E if has_sink else jnp.zeros((N_HEADS,), F32)
    shift = jnp.maximum(score_bound, jnp.max(sink2)) if has_sink else score_bound
    aux = jnp.concatenate([sink2, shift[None], jnp.zeros((1,), F32)])
    args = [aux, q, k_ctx, v_ctx]
    specs = [pl.BlockSpec(memory_space=pltpu.SMEM),
             pl.BlockSpec((1, rows, Q_WIDTH), lambda bi, i: (bi, i, 0)),
             pl.BlockSpec((1, n_ctx, KV_WIDTH), lambda bi, i: (bi, 0, 0)),
             pl.BlockSpec((1, n_ctx, KV_WIDTH), lambda bi, i: (bi, 0, 0))]
    if has_lat:
        n_lat = k_lat.shape[1]
        args += [k_lat, v_lat]
        specs += [pl.BlockSpec((1, n_lat, KV_WIDTH), lambda bi, i: (bi, 0, 0)),
                  pl.BlockSpec((1, n_lat, KV_WIDTH), lambda bi, i: (bi, 0, 0))]
    name = "window_attn" if window else ("global_attn" if has_lat else "ctx_attn")

    def call(bounded):
        return pl.pallas_call(
            functools.partial(_attn_kernel, tq=tq, nsub=nsub, has_lat=has_lat, window=window, has_sink=has_sink,
                              bounded=bounded, seq_len=nq),
            grid=(b, nq // rows),
            in_specs=specs,
            out_specs=pl.BlockSpec((1, rows, Q_WIDTH), lambda bi, i: (bi, i, 0)),
            out_shape=jax.ShapeDtypeStruct((b, nq, Q_WIDTH), BF16),
            compiler_params=_params(),
            name=name if bounded else name + "_rowmax",
        )(*args)

    return lax.cond(score_bound + shift <= MAX_EXP2_SPAN, lambda: call(True), lambda: call(False))


FF_CHUNK = 1024


def _mix_kernel(x_ref, mod_ref, g1_ref, g2_ref, oa_ref, ob_ref, oc_ref, wg_ref, wa_ref, wb_ref, wc_ref, wo_ref,
                w1_ref, w2_ref, out_ref, *, nsub):
    mod = mod_ref[0]
    sub = x_ref.shape[1] // nsub

    def merge(t):
        rows = slice(t * sub, (t + 1) * sub)
        x = x_ref[0, rows]
        h = _modulate(x, g1_ref[...] * (1.0 + mod[1:2]), mod[0:1]).astype(BF16)
        y = None
        for k, (o_ref, w_ref) in enumerate(((oa_ref, wa_ref), (ob_ref, wb_ref), (oc_ref, wc_ref))):
            gate = jax.nn.sigmoid(_dot(h, wg_ref[:, k * D_MODEL:(k + 1) * D_MODEL]))
            term = gate * _dot(o_ref[0, rows], w_ref[...])
            y = term if y is None else y + term
        return x + mod[2:3] * _dot(y.astype(BF16), wo_ref[...])

    def mlp(t, x):
        h = _modulate(x, g2_ref[...] * (1.0 + mod[4:5]), mod[3:4]).astype(BF16)
        acc = None
        for k in range(D_FF // FF_CHUNK):
            a = jnp.maximum(_dot(h, w1_ref[:, k * FF_CHUNK:(k + 1) * FF_CHUNK]), 0.0)
            part = _dot((a * a).astype(BF16), w2_ref[k * FF_CHUNK:(k + 1) * FF_CHUNK, :])
            acc = part if acc is None else acc + part
        out_ref[0, t * sub:(t + 1) * sub] = x + mod[5:6] * acc

    merged = [merge(t) for t in range(nsub)]
    for t, x in enumerate(merged):
        mlp(t, x)


def _mix(x, mod, gain1, gain2, oa, ob, oc, w_gates, w_a, w_b, w_c, w_out, w1, w2, *, layer, tm, nsub=1):
    b, seq_len, _ = x.shape
    mod_map = (lambda bi, i: (bi, 0, 0)) if mod.shape[0] == b else (lambda bi, i: (0, 0, 0))
    tok = lambda width: pl.BlockSpec((1, tm, width), lambda bi, i: (bi, i, 0))
    return pl.pallas_call(
        functools.partial(_mix_kernel, nsub=nsub),
        grid=(b, seq_len // tm),
        in_specs=[tok(D_MODEL), pl.BlockSpec((1, 6, D_MODEL), mod_map), _const_spec((1, D_MODEL)),
                  _const_spec((1, D_MODEL)), tok(Q_WIDTH), tok(POOL_WIDTH), tok(Q_WIDTH),
                  _const_spec((D_MODEL, 3 * D_MODEL), layer), _const_spec((Q_WIDTH, D_MODEL), layer),
                  _const_spec((POOL_WIDTH, D_MODEL), layer), _const_spec((Q_WIDTH, D_MODEL), layer),
                  _const_spec((D_MODEL, D_MODEL), layer), _const_spec((D_MODEL, D_FF), layer),
                  _const_spec((D_FF, D_MODEL), layer)],
        out_specs=tok(D_MODEL),
        out_shape=jax.ShapeDtypeStruct(x.shape, F32),
        compiler_params=_params(),
        name="mix",
    )(x, mod, gain1, gain2, oa, ob, oc, w_gates, w_a, w_b, w_c, w_out, w1, w2)


def _rope_tables(n_tok):
    rows = n_tok // GRID_W
    r = jnp.repeat(jnp.arange(rows, dtype=F32), GRID_W)
    col = jnp.tile(jnp.arange(GRID_W, dtype=F32), rows)
    inv = 1.0 / (ROPE_THETA ** (jnp.arange(0, AXIS_DIM, 2, dtype=F32) / AXIS_DIM))
    ang = jnp.concatenate([r[:, None] * inv, col[:, None] * inv], axis=-1)
    pair = np.arange(LANES) % 32
    sign = np.where(np.arange(LANES) < LANES // 2, -1.0, 1.0).astype(np.float32)
    return jnp.cos(ang)[:, pair], jnp.sin(ang)[:, pair] * sign


def kernel(x, c, ctx, c_ctx, w_ada, b_ada, norm1, norm2, w_in, q_norm_a, k_norm_a, q_norm_c, k_norm_c, sink_c,
           w_pool, pool_scale, w_br_a, w_br_b, w_br_c, w_out, w_mlp1, w_mlp2):
    batch, seq_len, _ = x.shape
    n_ctx = ctx.shape[1]
    pad = (-(batch + 1)) % 8
    v = jnp.concatenate([c, c_ctx[None], jnp.zeros((pad, D_MODEL), F32)], axis=0)
    mods = _adaln(v, w_ada, b_ada)

    cos_lat, sin_lat = _rope_tables(seq_len)
    cos_ctx = jnp.ones((n_ctx, LANES), F32)
    sin_ctx = jnp.zeros((n_ctx, LANES), F32)
    q_scale = HEAD_DIM ** -0.5 * LOG2E

    per_layer = lambda f: jnp.stack([f(l) for l in range(DEPTH)])
    w_proj = per_layer(lambda l: w_in[l][:, _PROJ_COLS].astype(BF16))
    w_gates = w_in[:, :, _GATES:].astype(BF16)
    wa = per_layer(lambda l: w_br_a[l][_OP].astype(BF16))
    wb = w_br_b.astype(BF16)
    wc = per_layer(lambda l: w_br_c[l][_OP].astype(BF16))
    wo = w_out.astype(BF16)
    w1 = w_mlp1.astype(BF16)
    w2 = w_mlp2.astype(BF16)

    xc = ctx
    for l in range(DEPTH):
        mod_lat = mods[l, :batch].reshape(batch, 6, D_MODEL)
        mod_ctx = mods[l, batch:batch + 1].reshape(1, 6, D_MODEL)
        head_gains = jnp.concatenate(
            [(q_norm_a[l] * q_scale)[_GP]] * N_GROUP + [(q_norm_c[l] * q_scale)[_GP]] * N_GROUP
            + [k_norm_a[l][_GP], k_norm_c[l][_GP]])[None]
        w_pool_bd = jax.scipy.linalg.block_diag(*[w_pool[l, g] for g in range(w_pool.shape[1])]).astype(BF16)
        ps = pool_scale[l][None]
        g1 = norm1[l][None]
        g2 = norm2[l][None]

        cqa, cqc, cka, ckc, cva, cvc, cob = _project(
            xc, mod_ctx, g1, w_proj, head_gains, cos_ctx, sin_ctx, w_pool_bd, ps, layer=l, tm=n_ctx)
        qa, qc, ka, kc, va, vc, ob = _project(
            x, mod_lat, g1, w_proj, head_gains, cos_lat, sin_lat, w_pool_bd, ps, layer=l, tm=1024, nsub=2)

        bound_a = _score_bound(q_norm_a[l] * q_scale, k_norm_a[l])
        bound_c = _score_bound(q_norm_c[l] * q_scale, k_norm_c[l])
        oa = _attention(qa, cka, cva, ka, va, score_bound=bound_a, nsub=4)
        oc = _attention(qc, ckc, cvc, kc, vc, sink_c[l], score_bound=bound_c, window=True, nsub=4)
        x = _mix(x, mod_lat, g1, g2, oa, ob, oc, w_gates, wa, wb, wc, wo, w1, w2, layer=l, tm=512, nsub=2)

        if l < DEPTH - 1:
            coa = _attention(cqa, cka, cva, score_bound=bound_a)
            coc = _attention(cqc, ckc, cvc, sink=sink_c[l], score_bound=bound_c)
            flat = lambda a: a.reshape(1, batch * n_ctx, a.shape[-1])
            xc = _mix(flat(xc), mod_ctx, g1, g2, flat(coa), flat(cob), flat(coc), w_gates, wa, wb, wc, wo, w1, w2,
                      layer=l, tm=512, nsub=2).reshape(ctx.shape)
    return x
```
